```python
import math
import jax, jax.numpy as jnp
from jax import lax
import numpy as np

D_MODEL = 1024
BATCH = 32
SEQ = 2048
DEPTH = 1
DEC_BATCH = 128
DEC_SEQ = 4
PAST_LEN = 8192
PAGE_SIZE = 128

HEAD_DIM = 64
GROUP_HEADS = 4
ATT_GROUPS = ((128, 1), (512, 4), (2048, 16))
N_GROUPS = len(ATT_GROUPS)
N_ATT_HEADS = N_GROUPS * GROUP_HEADS
ATT_W = N_ATT_HEADS * HEAD_DIM
ATT_OUT_W = GROUP_HEADS * HEAD_DIM
BLK = 128
N_BUCKETS = 32
MAX_DISTANCE = 2048
SGU_CHUNK = 128
SGU_GROUPS = 4
SGU_GROUP_W = 128
SGU_W = SGU_GROUPS * SGU_GROUP_W
D_FF = 2816
LN_EPS = 1e-5
ALPHA = (2 * DEPTH) ** 0.25
BETA = (8 * DEPTH) ** -0.25
IN_W = 3 * ATT_W + 2 * SGU_W + 2 * D_MODEL
NEG = -1e30
SCALE = HEAD_DIM ** -0.5

kernel_name = 'dilated_attn_gmlp_gated_macaron_step'


def _t5_bucket(dist):
    dist = np.asarray(dist, np.int64)
    max_exact = N_BUCKETS // 2
    large = max_exact + (np.log(np.maximum(dist, max_exact) / max_exact)
                         / np.log(MAX_DISTANCE / max_exact) * (N_BUCKETS - max_exact)).astype(np.int64)
    large = np.minimum(large, N_BUCKETS - 1)
    return np.where(dist < max_exact, dist, large).astype(np.int32)


def layer_norm(x, g, b):
    xf = x.astype(jnp.float32)
    mu = jnp.mean(xf, axis=-1, keepdims=True)
    var = jnp.mean(jnp.square(xf - mu), axis=-1, keepdims=True)
    return ((xf - mu) * lax.rsqrt(var + LN_EPS) * g.astype(jnp.float32) + b.astype(jnp.float32)).astype(x.dtype)


def swiglu(x, w1, w3, w2):
    return (jax.nn.silu(x @ w1) * (x @ w3)) @ w2


def dilated_attn_prompt(q, k, v, table, window, dil):
    B, S, H, E = q.shape
    L = S // dil
    band = window // dil
    assert band <= BLK
    nb = -(-L // BLK)
    Lp = nb * BLK

    def sub(a):
        return a.reshape(B, L, dil, H, E).transpose(0, 2, 3, 1, 4)

    qs = jnp.pad(sub(q), ((0, 0), (0, 0), (0, 0), (0, Lp - L), (0, 0))).reshape(B, dil, H, nb, BLK, E)

    def kblocks(a):
        ap = jnp.pad(sub(a), ((0, 0), (0, 0), (0, 0), (BLK, Lp - L), (0, 0)))
        prev = ap[:, :, :, :Lp].reshape(B, dil, H, nb, BLK, E)
        cur = ap[:, :, :, BLK:].reshape(B, dil, H, nb, BLK, E)
        return jnp.concatenate([prev, cur], axis=-2)

    kb, vb = kblocks(k), kblocks(v)
    rel = np.arange(BLK)[:, None] - np.arange(2 * BLK)[None, :] + BLK
    valid = (rel >= 0) & (rel <= band)
    bias = table.astype(jnp.float32)[_t5_bucket(np.clip(rel, 0, band) * dil)].transpose(2, 0, 1)
    kvalid = ~((np.arange(nb)[:, None] == 0) & (np.arange(2 * BLK)[None, :] < BLK))
    mask = valid[None] & kvalid[:, None, :]
    s = jnp.einsum('bdhnqe,bdhnke->bdhnqk', qs.astype(jnp.float32), kb.astype(jnp.float32)) * SCALE + bias[:, None]
    s = jnp.where(mask, s, NEG)
    m = jnp.max(s, axis=-1, keepdims=True)
    p = jnp.exp(s - m)
    l = jnp.sum(p, axis=-1, keepdims=True)
    o = jnp.einsum('bdhnqk,bdhnke->bdhnqe', p, vb.astype(jnp.float32)) / l
    lse = (m + jnp.log(l))[..., 0]
    o = o.reshape(B, dil, H, Lp, E)[:, :, :, :L].transpose(0, 3, 1, 2, 4).reshape(B, S, H, E)
    lse = lse.reshape(B, dil, H, Lp)[..., :L].transpose(0, 3, 1, 2).reshape(B, S, H)
    return o, lse


def dilated_attn_sample(q, k, v, kv_buf, table, window, dil):
    Bd, T, H, E = q.shape
    Lg = kv_buf.shape[1]
    band = window // dil
    full = jnp.concatenate([kv_buf, jnp.stack([k, v], axis=2).astype(kv_buf.dtype)], axis=1)
    steps = np.arange(band + 1)
    idx = Lg + np.arange(T)[:, None] - steps[None, :] * dil
    valid = idx >= 0
    kvg = full[:, np.maximum(idx, 0)]
    bias = table.astype(jnp.float32)[_t5_bucket(steps * dil)]
    s = jnp.einsum('bthe,btkhe->bthk', q.astype(jnp.float32), kvg[:, :, :, 0].astype(jnp.float32)) * SCALE + bias.T
    s = jnp.where(valid[:, None, :], s, NEG)
    m = jnp.max(s, axis=-1, keepdims=True)
    p = jnp.exp(s - m)
    l = jnp.sum(p, axis=-1, keepdims=True)
    o = jnp.einsum('bthk,btkhe->bthe', p, kvg[:, :, :, 1].astype(jnp.float32)) / l
    lse = (m + jnp.log(l))[..., 0]
    keep = min(window, full.shape[1])
    return o, lse, full[:, full.shape[1] - keep:]


def combine_groups(outs, lses):
    w = jax.nn.softmax(jnp.stack(lses, axis=-1), axis=-1)
    o = jnp.sum(jnp.stack(outs, axis=-2) * w[..., None], axis=-2)
    return o.reshape(*o.shape[:-2], ATT_OUT_W)


def sgu_branch(uv, ln_g, ln_b, ws, sb):
    Bn, L, _ = uv.shape
    z = jax.nn.gelu(uv, approximate=False)
    u, vr = z[..., :SGU_W], z[..., SGU_W:]
    vn = layer_norm(vr, ln_g, ln_b)
    cl = min(SGU_CHUNK, L)
    vc = vn.reshape(Bn, L // cl, cl, SGU_GROUPS, SGU_GROUP_W)
    wm = jnp.tril(ws)[:, :cl, :cl]
    mixed = jnp.einsum('gpq,bnqgc->bnpgc', wm, vc) + sb[:, :cl].T[:, :, None]
    return u * mixed.reshape(Bn, L, SGU_W), vn


def trunk_layer(x, rel_bias, kv_bufs, ln1_g, ln1_b, f1_w1, f1_w3, f1_w2, w_in, b_in,
                sgu_ln_g, sgu_ln_b, sgu_ws, sgu_b, w_oa, w_ob, w_out,
                ln2_g, ln2_b, f2_w1, f2_w3, f2_w2, ln3_g, ln3_b):
    h = layer_norm(ALPHA * x + 0.5 * swiglu(x, f1_w1, f1_w3, f1_w2), ln1_g, ln1_b)
    z = h @ w_in + b_in
    lead = z.shape[:-1]
    q = z[..., :ATT_W].reshape(*lead, N_GROUPS, GROUP_HEADS, HEAD_DIM)
    k = z[..., ATT_W:2 * ATT_W].reshape(*lead, N_GROUPS, GROUP_HEADS, HEAD_DIM)
    v = z[..., 2 * ATT_W:3 * ATT_W].reshape(*lead, N_GROUPS, GROUP_HEADS, HEAD_DIM)
    o_uv = 3 * ATT_W
    uv = z[..., o_uv:o_uv + 2 * SGU_W]
    ga = z[..., o_uv + 2 * SGU_W:o_uv + 2 * SGU_W + D_MODEL]
    gb = z[..., o_uv + 2 * SGU_W + D_MODEL:]
    outs, lses, new_bufs = [], [], []
    for g, (window, dil) in enumerate(ATT_GROUPS):
        tab = rel_bias[:, g * GROUP_HEADS:(g + 1) * GROUP_HEADS]
        qg, kg, vg = q[..., g, :, :], k[..., g, :, :], v[..., g, :, :]
        if kv_bufs is None:
            o, lse = dilated_attn_prompt(qg, kg, vg, tab, window, dil)
            S = x.shape[1]
            keep = min(window, S)
            nbuf = jnp.stack([kg, vg], axis=2)[:, S - keep:]
        else:
            o, lse, nbuf = dilated_attn_sample(qg, kg, vg, kv_bufs[g], tab, window, dil)
        outs.append(o)
        lses.append(lse)
        new_bufs.append(nbuf)
    att = combine_groups(outs, lses).astype(h.dtype)
    yb, vrows = sgu_branch(uv, sgu_ln_g, sgu_ln_b, sgu_ws, sgu_b)
    mix = (jax.nn.sigmoid(ga) * (att @ w_oa) + jax.nn.sigmoid(gb) * (yb @ w_ob)) @ w_out
    h = layer_norm(ALPHA * h + mix, ln2_g, ln2_b)
    h = layer_norm(ALPHA * h + 0.5 * swiglu(h, f2_w1, f2_w3, f2_w2), ln3_g, ln3_b)
    return h, new_bufs, vrows


def setup_inputs(seed: int = 0) -> dict:
    key = jax.random.key(seed)
    ks = iter(jax.random.split(key, 40))

    def nrm(shape, scale):
        return jax.random.normal(next(ks), shape, jnp.float32) * scale

    def gain(shape):
        return 1.0 + nrm(shape, 0.05)

    inp = {}
    inp['x_prompt'] = nrm((BATCH, SEQ, D_MODEL), 1.0)
    inp['x_sample'] = nrm((DEC_BATCH, DEC_SEQ, D_MODEL), 1.0)
    for g, (window, dil) in enumerate(ATT_GROUPS):
        inp['state_win%d' % g] = nrm((DEPTH, DEC_BATCH, min(window, PAST_LEN), 2, GROUP_HEADS, HEAD_DIM), 1.0)
    inp['rel_bias'] = nrm((N_BUCKETS, N_ATT_HEADS), 0.3)
    inp['ln1_g'] = gain((DEPTH, D_MODEL))
    inp['ln1_b'] = nrm((DEPTH, D_MODEL), 0.05)
    inp['f1_w1'] = nrm((DEPTH, D_MODEL, D_FF), D_MODEL ** -0.5)
    inp['f1_w3'] = nrm((DEPTH, D_MODEL, D_FF), D_MODEL ** -0.5)
    inp['f1_w2'] = nrm((DEPTH, D_FF, D_MODEL), BETA * D_FF ** -0.5)
    inp['w_in'] = nrm((DEPTH, D_MODEL, IN_W), D_MODEL ** -0.5)
    inp['b_in'] = nrm((DEPTH, IN_W), 0.02)
    inp['sgu_ln_g'] = gain((DEPTH, SGU_W))
    inp['sgu_ln_b'] = nrm((DEPTH, SGU_W), 0.05)
    inp['sgu_ws'] = nrm((DEPTH, SGU_GROUPS, SGU_CHUNK, SGU_CHUNK), 0.5 * SGU_CHUNK ** -0.5)
    inp['sgu_b'] = gain((DEPTH, SGU_GROUPS, SGU_CHUNK))
    inp['w_oa'] = nrm((DEPTH, ATT_OUT_W, D_MODEL), BETA * ATT_OUT_W ** -0.5)
    inp['w_ob'] = nrm((DEPTH, SGU_W, D_MODEL), BETA * SGU_W ** -0.5)
    inp['w_out'] = nrm((DEPTH, D_MODEL, D_MODEL), BETA * D_MODEL ** -0.5)
    inp['ln2_g'] = gain((DEPTH, D_MODEL))
    inp['ln2_b'] = nrm((DEPTH, D_MODEL), 0.05)
    inp['f2_w1'] = nrm((DEPTH, D_MODEL, D_FF), D_MODEL ** -0.5)
    inp['f2_w3'] = nrm((DEPTH, D_MODEL, D_FF), D_MODEL ** -0.5)
    inp['f2_w2'] = nrm((DEPTH, D_FF, D_MODEL), BETA * D_FF ** -0.5)
    inp['ln3_g'] = gain((DEPTH, D_MODEL))
    inp['ln3_b'] = nrm((DEPTH, D_MODEL), 0.05)
    return inp


def reference(x_prompt, x_sample, state_win0, state_win1, state_win2, rel_bias,
              ln1_g, ln1_b, f1_w1, f1_w3, f1_w2, w_in, b_in,
              sgu_ln_g, sgu_ln_b, sgu_ws, sgu_b, w_oa, w_ob, w_out,
              ln2_g, ln2_b, f2_w1, f2_w3, f2_w2, ln3_g, ln3_b):
    yp, ys = x_prompt, x_sample
    pw0, pw1, pw2, sw0, sw1, sw2, sv = [], [], [], [], [], [], []
    for l in range(DEPTH):
        lw = (ln1_g[l], ln1_b[l], f1_w1[l], f1_w3[l], f1_w2[l], w_in[l], b_in[l],
              sgu_ln_g[l], sgu_ln_b[l], sgu_ws[l], sgu_b[l], w_oa[l], w_ob[l], w_out[l],
              ln2_g[l], ln2_b[l], f2_w1[l], f2_w3[l], f2_w2[l], ln3_g[l], ln3_b[l])
        yp, nbp, _ = trunk_layer(yp, rel_bias, None, *lw)
        ys, nbs, vrows = trunk_layer(ys, rel_bias, (state_win0[l], state_win1[l], state_win2[l]), *lw)
        pw0.append(nbp[0]); pw1.append(nbp[1]); pw2.append(nbp[2])
        sw0.append(nbs[0]); sw1.append(nbs[1]); sw2.append(nbs[2])
        sv.append(vrows)
    return (yp, ys, jnp.stack(pw0), jnp.stack(pw1), jnp.stack(pw2),
            jnp.stack(sw0), jnp.stack(sw1), jnp.stack(sw2), jnp.stack(sv))
```

```python
import functools
import math

import jax
import jax.numpy as jnp
import numpy as np
from jax import lax
from jax.experimental import pallas as pl
from jax.experimental.pallas import tpu as pltpu

F32 = jnp.float32
BF16 = jnp.bfloat16

D_MODEL = 1024
HEAD_DIM = 64
GROUP_HEADS = 4
ATT_GROUPS = ((128, 1), (512, 4), (2048, 16))
N_GROUPS = len(ATT_GROUPS)
N_ATT_HEADS = N_GROUPS * GROUP_HEADS
ATT_W = N_ATT_HEADS * HEAD_DIM
GW = GROUP_HEADS * HEAD_DIM
BLK = 128
N_BUCKETS = 32
MAX_DISTANCE = 2048
SGU_CHUNK = 128
SGU_GROUPS = 4
SGU_GROUP_W = 128
SGU_W = SGU_GROUPS * SGU_GROUP_W
D_FF = 2816
LN_EPS = 1e-5
NEG = -1e30
SCALE = HEAD_DIM ** -0.5

TM = 512
FC = 256
NC = D_FF // FC
VMEM_LIMIT = 56 * 1024 * 1024


def _t5_bucket(dist):
    dist = np.asarray(dist, np.int64)
    max_exact = N_BUCKETS // 2
    large = max_exact + (np.log(np.maximum(dist, max_exact) / max_exact)
                         / np.log(MAX_DISTANCE / max_exact) * (N_BUCKETS - max_exact)).astype(np.int64)
    large = np.minimum(large, N_BUCKETS - 1)
    return np.where(dist < max_exact, dist, large).astype(np.int32)


def _ln(t, g, b):
    mu = jnp.mean(t, axis=-1, keepdims=True)
    d = t - mu
    var = jnp.mean(d * d, axis=-1, keepdims=True)
    return d * lax.rsqrt(var + LN_EPS) * g + b


def _params(n_axes):
    return pltpu.CompilerParams(dimension_semantics=("arbitrary",) * n_axes,
                                vmem_limit_bytes=VMEM_LIMIT)


def _whole(shape):
    nd = len(shape)
    return pl.BlockSpec(shape, lambda *_: (0,) * nd)


def _ffn_ln_kernel(alpha, x_ref, w13_ref, w2_ref, g_ref, b_ref, o_ref, xb_ref, acc_ref):
    xb_ref[...] = x_ref[...].astype(BF16)
    acc_ref[...] = jnp.zeros_like(acc_ref)

    def body(c, carry):
        ab = jnp.dot(xb_ref[...], w13_ref[c], preferred_element_type=F32)
        a = ab[:, :FC]
        b = ab[:, FC:]
        hid = (a * jax.nn.sigmoid(a)) * b
        acc_ref[...] += jnp.dot(hid.astype(BF16), w2_ref[c], preferred_element_type=F32)
        return carry

    lax.fori_loop(0, NC, body, 0)
    t = alpha * x_ref[...] + 0.5 * acc_ref[...]
    o_ref[...] = _ln(t, g_ref[...], b_ref[...])


def _ffn_ln(x2d, w13, w2, g, b, alpha):
    n = x2d.shape[0]
    assert n % TM == 0
    return pl.pallas_call(
        functools.partial(_ffn_ln_kernel, alpha),
        grid=(n // TM,),
        in_specs=[pl.BlockSpec((TM, D_MODEL), lambda i: (i, 0)),
                  _whole((NC, D_MODEL, 2 * FC)), _whole((NC, FC, D_MODEL)),
                  _whole((1, D_MODEL)), _whole((1, D_MODEL))],
        out_specs=pl.BlockSpec((TM, D_MODEL), lambda i: (i, 0)),
        out_shape=jax.ShapeDtypeStruct((n, D_MODEL), F32),
        scratch_shapes=[pltpu.VMEM((TM, D_MODEL), BF16), pltpu.VMEM((TM, D_MODEL), F32)],
        compiler_params=_params(1),
        name="ffn_ln",
    )(x2d, w13, w2, g, b)


def _gelu_exact(z):
    return 0.5 * z * (1.0 + lax.erf(z * (1.0 / math.sqrt(2.0))))


def _uv_branch(hb, wuv_ref, buv_ref, sg_ref, sb_ref):
    zuv = jnp.dot(hb, wuv_ref[...], preferred_element_type=F32) + buv_ref[...]
    zz = _gelu_exact(zuv)
    u = zz[:, :SGU_W]
    vn = _ln(zz[:, SGU_W:], sg_ref[...], sb_ref[...])
    return u, vn


def _gates(hb, wg_ref, bg_ref, sg_out_ref):
    zg = jnp.dot(hb, wg_ref[...], preferred_element_type=F32) + bg_ref[...]
    sg_out_ref[...] = jax.nn.sigmoid(zg).astype(BF16)


def _proj_prompt_kernel(h_ref, wqkv_ref, bqkv_ref, wuv_ref, buv_ref, wg_ref, bg_ref,
                        sln_g_ref, sln_b_ref, ws_ref, sbt_ref,
                        qkv0_ref, qkv1_ref, qkv2_ref, win0_ref, win1_ref, win2_ref,
                        yb_ref, sg_ref, zs_ref):
    hb = h_ref[...].astype(BF16)
    qkv_refs = (qkv0_ref, qkv1_ref, qkv2_ref)
    win_refs = (win0_ref, win1_ref, win2_ref)
    for g, (window, dil) in enumerate(ATT_GROUPS):
        zg = jnp.dot(hb, wqkv_ref[g], preferred_element_type=F32) + bqkv_ref[g]
        keep = min(window, TM)
        win_refs[g][0] = zg[TM - keep:, GW:]
        zq = jnp.concatenate([zg[:, :GW] * SCALE, zg[:, GW:]], axis=1)
        if dil == 1:
            qkv_refs[g][0, 0] = zq.astype(BF16)
            continue
        ncol = 3 * GW // BLK
        for j in range(ncol):
            zs_ref[j] = zq[:, j * BLK:(j + 1) * BLK]
        rows = TM // dil
        for r in range(dil):
            sub = jnp.concatenate([zs_ref[j, pl.ds(r, rows, stride=dil), :] for j in range(ncol)], axis=1)
            qkv_refs[g][0, r] = sub.astype(BF16)

    u, vn = _uv_branch(hb, wuv_ref, buv_ref, sln_g_ref, sln_b_ref)
    vnb = vn.astype(BF16)
    row = lax.broadcasted_iota(jnp.int32, (SGU_CHUNK, SGU_CHUNK), 0)
    col = lax.broadcasted_iota(jnp.int32, (SGU_CHUNK, SGU_CHUNK), 1)
    for g in range(SGU_GROUPS):
        wm = jnp.where(row >= col, ws_ref[g], 0.0).astype(BF16)
        bias = jnp.broadcast_to(sbt_ref[:, g:g + 1], (SGU_CHUNK, SGU_GROUP_W))
        cs = slice(g * SGU_GROUP_W, (g + 1) * SGU_GROUP_W)
        for c in range(TM // SGU_CHUNK):
            rs = slice(c * SGU_CHUNK, (c + 1) * SGU_CHUNK)
            mixed = jnp.dot(wm, vnb[rs, cs], preferred_element_type=F32) + bias
            yb_ref[rs, cs] = (u[rs, cs] * mixed).astype(BF16)

    _gates(hb, wg_ref, bg_ref, sg_ref)


def _proj_prompt(h2d, batch, seq, wqkv, bqkv, wuv, buv, wg, bg, sln_g, sln_b, ws, sbt):
    nt = seq // TM
    assert seq % TM == 0 and TM == ATT_GROUPS[1][0] and TM >= ATT_GROUPS[0][0]
    in_specs = [pl.BlockSpec((TM, D_MODEL), lambda b, t: (b * nt + t, 0)),
                _whole(wqkv.shape), _whole(bqkv.shape), _whole(wuv.shape), _whole(buv.shape),
                _whole(wg.shape), _whole(bg.shape), _whole(sln_g.shape), _whole(sln_b.shape),
                _whole(ws.shape), _whole(sbt.shape)]
    out_shape, out_specs = [], []
    for window, dil in ATT_GROUPS:
        out_shape.append(jax.ShapeDtypeStruct((batch, dil, seq // dil, 3 * GW), BF16))
        out_specs.append(pl.BlockSpec((1, dil, TM // dil, 3 * GW), lambda b, t: (b, 0, t, 0)))
    for window, dil in ATT_GROUPS:
        keep = min(window, seq)
        out_shape.append(jax.ShapeDtypeStruct((batch, keep, 2 * GW), F32))
        if keep > TM:
            assert keep == seq
            out_specs.append(pl.BlockSpec((1, TM, 2 * GW), lambda b, t: (b, t, 0)))
        else:
            out_specs.append(pl.BlockSpec((1, keep, 2 * GW), lambda b, t: (b, 0, 0)))
    out_shape += [jax.ShapeDtypeStruct((batch * seq, SGU_W), BF16),
                  jax.ShapeDtypeStruct((batch * seq, 2 * D_MODEL), BF16)]
    out_specs += [pl.BlockSpec((TM, SGU_W), lambda b, t: (b * nt + t, 0)),
                  pl.BlockSpec((TM, 2 * D_MODEL), lambda b, t: (b * nt + t, 0))]
    return pl.pallas_call(
        _proj_prompt_kernel,
        grid=(batch, nt),
        in_specs=in_specs,
        out_specs=out_specs,
        out_shape=out_shape,
        scratch_shapes=[pltpu.VMEM((3 * GW // BLK, TM, BLK), F32)],
        compiler_params=_params(2),
        name="proj_prompt",
    )(h2d, wqkv, bqkv, wuv, buv, wg, bg, sln_g, sln_b, ws, sbt)


def _proj_sample_kernel(t_len, ws4_ref, sb4_ref, h_ref, wqkv_ref, bqkv_ref, wuv_ref, buv_ref,
                        wg_ref, bg_ref, sln_g_ref, sln_b_ref,
                        q_ref, kv_ref, yb_ref, sg_ref, vn_ref):
    n = h_ref.shape[0]
    hb = h_ref[...].astype(BF16)
    for g in range(N_GROUPS):
        zg = jnp.dot(hb, wqkv_ref[g], preferred_element_type=F32) + bqkv_ref[g]
        q_ref[:, g * GW:(g + 1) * GW] = zg[:, :GW] * SCALE
        kv_ref[:, g * 2 * GW:(g + 1) * 2 * GW] = zg[:, GW:]

    u, vn = _uv_branch(hb, wuv_ref, buv_ref, sln_g_ref, sln_b_ref)
    vn_ref[...] = vn
    p = lax.broadcasted_iota(jnp.int32, (n, SGU_GROUP_W), 0) & (t_len - 1)
    for g in range(SGU_GROUPS):
        cs = slice(g * SGU_GROUP_W, (g + 1) * SGU_GROUP_W)
        vg = vn[:, cs]
        acc = jnp.zeros((n, SGU_GROUP_W), F32)
        for pp in range(t_len):
            acc = jnp.where(p == pp, sb4_ref[g, pp], acc)
        for d in range(t_len):
            coef = jnp.zeros((n, SGU_GROUP_W), F32)
            for pp in range(d, t_len):
                coef = jnp.where(p == pp, ws4_ref[g, pp * t_len + (pp - d)], coef)
            shifted = vg if d == 0 else pltpu.roll(vg, d, axis=0)
            acc = acc + coef * shifted
        yb_ref[:, cs] = (u[:, cs] * acc).astype(BF16)

    _gates(hb, wg_ref, bg_ref, sg_ref)


def _proj_sample(h2d, t_len, ws4, sb4, wqkv, bqkv, wuv, buv, wg, bg, sln_g, sln_b):
    n = h2d.shape[0]
    assert t_len & (t_len - 1) == 0
    smem = pl.BlockSpec(memory_space=pltpu.SMEM)
    ins = (h2d, wqkv, bqkv, wuv, buv, wg, bg, sln_g, sln_b)
    out_shape = [jax.ShapeDtypeStruct((n, N_GROUPS * GW), F32),
                 jax.ShapeDtypeStruct((n, N_GROUPS * 2 * GW), F32),
                 jax.ShapeDtypeStruct((n, SGU_W), BF16),
                 jax.ShapeDtypeStruct((n, 2 * D_MODEL), BF16),
                 jax.ShapeDtypeStruct((n, SGU_W), F32)]
    return pl.pallas_call(
        functools.partial(_proj_sample_kernel, t_len),
        grid=(1,),
        in_specs=[smem, smem] + [_whole(a.shape) for a in ins],
        out_specs=[_whole(s.shape) for s in out_shape],
        out_shape=out_shape,
        compiler_params=_params(1),
        name="proj_sample",
    )(ws4, sb4, *ins)


def _bias_mask(bmap, buckets, value_of_bucket):
    acc = jnp.full(bmap.shape, NEG, F32)
    for bkt in buckets:
        acc = jnp.where(bmap == bkt, value_of_bucket(bkt), acc)
    return acc


def _prompt_bucket_map(window, dil):
    band = window // dil
    rel = np.arange(BLK)[:, None] - np.arange(2 * BLK)[None, :] + BLK
    valid = (rel >= 0) & (rel <= band)
    return np.where(valid, _t5_bucket(np.clip(rel, 0, band) * dil), -1).astype(np.int32)


def _attn_prompt_kernel(buckets, tab_ref, bmap_ref, qkv0_ref, qkv1_ref, qkv2_ref,
                        ol0_ref, ol1_ref, ol2_ref, bm_ref):
    @pl.when(pl.program_id(0) == 0)
    def _():
        for g in range(N_GROUPS):
            bmap = bmap_ref[g]
            for h in range(GROUP_HEADS):
                bm_ref[g, h * BLK:(h + 1) * BLK, :] = _bias_mask(
                    bmap, buckets[g], lambda bkt: tab_ref[bkt, g * GROUP_HEADS + h])

    lane_head = lax.broadcasted_iota(jnp.int32, (BLK, GW), 1) >> 6

    def unit(g, qkv_ref, ol_ref, r, q0, first):
        q = qkv_ref[0, r, pl.ds(q0, BLK), 0:GW]
        qs = jnp.concatenate([jnp.where(lane_head == h, q, jnp.zeros_like(q))
                              for h in range(GROUP_HEADS)], axis=0)
        if first:
            k = qkv_ref[0, r, pl.ds(q0, BLK), GW:2 * GW]
            v = qkv_ref[0, r, pl.ds(q0, BLK), 2 * GW:3 * GW]
            bm = bm_ref[g, :, BLK:]
        else:
            k0 = pl.multiple_of(q0 - BLK, BLK)
            k = qkv_ref[0, r, pl.ds(k0, 2 * BLK), GW:2 * GW]
            v = qkv_ref[0, r, pl.ds(k0, 2 * BLK), 2 * GW:3 * GW]
            bm = bm_ref[g]
        s = lax.dot_general(qs, k, (((1,), (1,)), ((), ())), preferred_element_type=F32) + bm
        m = jnp.max(s, axis=-1, keepdims=True)
        p = jnp.exp(s - m)
        l = jnp.sum(p, axis=-1, keepdims=True)
        pv = jnp.dot(p.astype(BF16), v, preferred_element_type=F32) * (1.0 / l)
        lse = m + jnp.log(l)
        o = jnp.zeros((BLK, GW), F32)
        ls = jnp.zeros((BLK, GW), F32)
        for h in range(GROUP_HEADS):
            rs = slice(h * BLK, (h + 1) * BLK)
            o = jnp.where(lane_head == h, pv[rs], o)
            ls = jnp.where(lane_head == h, lse[rs], ls)
        ol_ref[0, r, pl.ds(q0, BLK), 0:GW] = o
        ol_ref[0, r, pl.ds(q0, BLK), GW:2 * GW] = ls

    for g, (qkv_ref, ol_ref) in enumerate(((qkv0_ref, ol0_ref), (qkv1_ref, ol1_ref), (qkv2_ref, ol2_ref))):
        dil, sub_len = qkv_ref.shape[1], qkv_ref.shape[2]
        nb = sub_len // BLK

        def per_residue(r, carry, g=g, qkv_ref=qkv_ref, ol_ref=ol_ref, nb=nb):
            unit(g, qkv_ref, ol_ref, r, 0, True)
            if nb > 1:
                def per_block(n, c2):
                    unit(g, qkv_ref, ol_ref, r, pl.multiple_of(n * BLK, BLK), False)
                    return c2
                lax.fori_loop(1, nb, per_block, 0)
            return carry

        if dil == 1:
            per_residue(0, 0)
        else:
            lax.fori_loop(0, dil, per_residue, 0)


def _attn_prompt(tab, qkvs, batch):
    bmaps = [_prompt_bucket_map(w, d) for w, d in ATT_GROUPS]
    buckets = tuple(tuple(int(b) for b in np.unique(m) if b >= 0) for m in bmaps)
    bmap = jnp.asarray(np.stack(bmaps))
    in_specs = [pl.BlockSpec(memory_space=pltpu.SMEM), _whole(bmap.shape)]
    out_shape, out_specs = [], []
    for a in qkvs:
        _, dil, sub_len, _ = a.shape
        in_specs.append(pl.BlockSpec((1, dil, sub_len, 3 * GW), lambda b: (b, 0, 0, 0)))
        out_shape.append(jax.ShapeDtypeStruct((batch, dil, sub_len, 2 * GW), F32))
        out_specs.append(pl.BlockSpec((1, dil, sub_len, 2 * GW), lambda b: (b, 0, 0, 0)))
    return pl.pallas_call(
        functools.partial(_attn_prompt_kernel, buckets),
        grid=(batch,),
        in_specs=in_specs,
        out_specs=out_specs,
        out_shape=out_shape,
        scratch_shapes=[pltpu.VMEM((N_GROUPS, GROUP_HEADS * BLK, 2 * BLK), F32)],
        compiler_params=_params(1),
        name="attn_prompt",
    )(tab, bmap, *qkvs)


def _attn_sample_kernel(g, t_len, buckets, tab_ref, bmap_ref, q_ref, kvnew_ref, st_ref,
                        win_ref, ol_ref, bm_ref, tr_ref):
    lg = st_ref.shape[2]
    nrow = GROUP_HEADS * t_len
    row_head = lax.broadcasted_iota(jnp.int32, (nrow, GW), 0) // t_len
    lane_head = lax.broadcasted_iota(jnp.int32, (nrow, GW), 1) >> 6

    @pl.when(pl.program_id(0) == 0)
    def _():
        rh = lax.broadcasted_iota(jnp.int32, bm_ref.shape, 0) // t_len

        def value(bkt):
            v = jnp.full(bm_ref.shape, tab_ref[bkt, g * GROUP_HEADS], F32)
            for h in range(1, GROUP_HEADS):
                v = jnp.where(rh == h, tab_ref[bkt, g * GROUP_HEADS + h], v)
            return v

        bm_ref[...] = _bias_mask(bmap_ref[...], buckets, value)
        tr_ref[...] = jnp.zeros_like(tr_ref)

    tr_ref[0:t_len, :] = kvnew_ref[0]
    tail = jnp.concatenate([tr_ref[:, j * BLK:(j + 1) * BLK].T for j in range(2 * GW // BLK)], axis=0)
    full = jnp.concatenate([st_ref[0], tail], axis=1)
    ext = lg + BLK
    win_ref[0] = pltpu.roll(full, ext - t_len, axis=1)[:, :lg]

    fb = full.astype(BF16)
    qs = jnp.where(row_head == lane_head, q_ref[0], 0.0).astype(BF16)
    s = jnp.dot(qs, fb[0:GW], preferred_element_type=F32) + bm_ref[...]
    m = jnp.max(s, axis=-1, keepdims=True)
    p = jnp.exp(s - m)
    l = jnp.sum(p, axis=-1, keepdims=True)
    pv = lax.dot_general(p.astype(BF16), fb[GW:2 * GW], (((1,), (1,)), ((), ())),
                         preferred_element_type=F32) * (1.0 / l)
    lse = jnp.broadcast_to(m + jnp.log(l), (nrow, GW))
    sel = row_head == lane_head
    o = jnp.where(sel, pv, 0.0)
    ls = jnp.where(sel, lse, 0.0)
    osum, lsum = o, ls
    for h in range(1, GROUP_HEADS):
        osum = osum + pltpu.roll(o, h * t_len, axis=0)
        lsum = lsum + pltpu.roll(ls, h * t_len, axis=0)
    ol_ref[0, :, 0:GW] = osum[0:t_len]
    ol_ref[0, :, GW:2 * GW] = lsum[0:t_len]


def _sample_bucket_map(window, dil, lg, t_len):
    band = window // dil
    ext = lg + BLK
    bmap = np.full((t_len, ext), -1, np.int64)
    t = np.arange(t_len)[:, None]
    j = np.arange(lg + t_len)[None, :]
    dist = lg + t - j
    valid = (dist >= 0) & (dist % dil == 0) & (dist <= band * dil)
    bmap[:, :lg + t_len] = np.where(valid, _t5_bucket(np.maximum(dist, 0)), -1)
    return np.tile(bmap, (GROUP_HEADS, 1)).astype(np.int32)


def _attn_sample(g, tab, q16, kvnew, st):
    window, dil = ATT_GROUPS[g]
    nb, _, lg = st.shape
    t_len = kvnew.shape[1]
    nrow = GROUP_HEADS * t_len
    bmap_np = _sample_bucket_map(window, dil, lg, t_len)
    buckets = tuple(int(b) for b in np.unique(bmap_np) if b >= 0)
    bmap = jnp.asarray(bmap_np)
    return pl.pallas_call(
        functools.partial(_attn_sample_kernel, g, t_len, buckets),
        grid=(nb,),
        in_specs=[pl.BlockSpec(memory_space=pltpu.SMEM), _whole(bmap.shape),
                  pl.BlockSpec((1, nrow, GW), lambda b: (b, 0, 0)),
                  pl.BlockSpec((1, t_len, 2 * GW), lambda b: (b, 0, 0)),
                  pl.BlockSpec((1, 2 * GW, lg), lambda b: (b, 0, 0))],
        out_specs=[pl.BlockSpec((1, 2 * GW, lg), lambda b: (b, 0, 0)),
                   pl.BlockSpec((1, t_len, 2 * GW), lambda b: (b, 0, 0))],
        out_shape=[jax.ShapeDtypeStruct((nb, 2 * GW, lg), F32),
                   jax.ShapeDtypeStruct((nb, t_len, 2 * GW), F32)],
        scratch_shapes=[pltpu.VMEM((nrow, lg + BLK), F32), pltpu.VMEM((BLK, 2 * GW), F32)],
        compiler_params=_params(1),
        name="attn_sample_g%d" % g,
    )(tab, bmap, q16, kvnew, st)


def _mix_kernel(alpha, h_ref, ol0_ref, ol1_ref, ol2_ref, yb_ref, sg_ref, woa_ref, wob_ref, wout_ref,
                g_ref, b_ref, o_ref, u_ref):
    tm = h_ref.shape[0]
    ncol = 2 * GW // BLK
    outs, lses = [], []
    for gi, ol_ref in enumerate((ol0_ref, ol1_ref, ol2_ref)):
        dil = ol_ref.shape[1]
        if dil == 1:
            ol = ol_ref[0, 0]
        else:
            for r in range(dil):
                for j in range(ncol):
                    u_ref[gi, j, pl.ds(r, tm // dil, stride=dil), :] = ol_ref[0, r, :, j * BLK:(j + 1) * BLK]
            ol = jnp.concatenate([u_ref[gi, j] for j in range(ncol)], axis=1)
        outs.append(ol[:, :GW])
        lses.append(ol[:, GW:])
    mx = jnp.maximum(jnp.maximum(lses[0], lses[1]), lses[2])
    es = [jnp.exp(ls - mx) for ls in lses]
    den = es[0] + es[1] + es[2]
    num = es[0] * outs[0] + es[1] * outs[1] + es[2] * outs[2]
    att = (num * (1.0 / den)).astype(BF16)
    a = jnp.dot(att, woa_ref[...], preferred_element_type=F32)
    bb = jnp.dot(yb_ref[...], wob_ref[...], preferred_element_type=F32)
    gated = sg_ref[:, :D_MODEL].astype(F32) * a + sg_ref[:, D_MODEL:].astype(F32) * bb
    mix = jnp.dot(gated.astype(BF16), wout_ref[...], preferred_element_type=F32)
    o_ref[...] = _ln(alpha * h_ref[...] + mix, g_ref[...], b_ref[...])


def _mix(h2d, ols, yb, sg, woa, wob, wout, g, b, alpha, batch, seq):
    tm = min(TM, seq)
    nt = seq // tm
    in_specs = [pl.BlockSpec((tm, D_MODEL), lambda bi, t: (bi * nt + t, 0))]
    for a in ols:
        dil = a.shape[1]
        in_specs.append(pl.BlockSpec((1, dil, tm // dil, 2 * GW), lambda bi, t: (bi, 0, t, 0)))
    in_specs += [pl.BlockSpec((tm, SGU_W), lambda bi, t: (bi * nt + t, 0)),
                 pl.BlockSpec((tm, 2 * D_MODEL), lambda bi, t: (bi * nt + t, 0)),
                 _whole(woa.shape), _whole(wob.shape), _whole(wout.shape),
                 _whole(g.shape), _whole(b.shape)]
    return pl.pallas_call(
        functools.partial(_mix_kernel, alpha),
        grid=(batch, nt),
        in_specs=in_specs,
        out_specs=pl.BlockSpec((tm, D_MODEL), lambda bi, t: (bi * nt + t, 0)),
        out_shape=jax.ShapeDtypeStruct((batch * seq, D_MODEL), F32),
        scratch_shapes=[pltpu.VMEM((N_GROUPS, 2 * GW // BLK, tm, BLK), F32)],
        compiler_params=_params(2),
        name="mix",
    )(h2d, *ols, yb, sg, woa, wob, wout, g, b)


def _prep_ffn(w1, w3, w2):
    w13 = jnp.concatenate([w1.reshape(D_MODEL, NC, FC), w3.reshape(D_MODEL, NC, FC)], axis=-1)
    return (jnp.transpose(w13, (1, 0, 2)).astype(BF16), w2.reshape(NC, FC, D_MODEL).astype(BF16))


def _prep_w_in(w_in, b_in):
    def cols(a, lo, n):
        return a[..., lo:lo + n]
    wq, bq = [], []
    for g in range(N_GROUPS):
        parts = [(i * ATT_W + g * GW, GW) for i in range(3)]
        wq.append(jnp.concatenate([cols(w_in, lo, n) for lo, n in parts], axis=-1))
        bq.append(jnp.concatenate([cols(b_in, lo, n) for lo, n in parts], axis=-1))
    o_uv = 3 * ATT_W
    return (jnp.stack(wq).astype(BF16), jnp.stack(bq)[:, None, :],
            cols(w_in, o_uv, 2 * SGU_W).astype(BF16), cols(b_in, o_uv, 2 * SGU_W)[None, :],
            cols(w_in, o_uv + 2 * SGU_W, 2 * D_MODEL).astype(BF16),
            cols(b_in, o_uv + 2 * SGU_W, 2 * D_MODEL)[None, :])


def _to_window(a):
    return a.reshape(a.shape[0], a.shape[1], 2, GROUP_HEADS, HEAD_DIM)


def kernel(x_prompt, x_sample, state_win0, state_win1, state_win2, rel_bias, ln1_g, ln1_b, f1_w1, f1_w3, f1_w2, w_in, b_in, sgu_ln_g, sgu_ln_b, sgu_ws, sgu_b, w_oa, w_ob, w_out, ln2_g, ln2_b, f2_w1, f2_w3, f2_w2, ln3_g, ln3_b):
    depth = ln1_g.shape[0]
    alpha = (2 * depth) ** 0.25
    batch, seq, _ = x_prompt.shape
    dec_batch, dec_seq, _ = x_sample.shape
    states = (state_win0, state_win1, state_win2)

    yp = x_prompt.reshape(batch * seq, D_MODEL)
    ys = x_sample.reshape(dec_batch * dec_seq, D_MODEL)
    outs = [[] for _ in range(7)]
    for l in range(depth):
        f1 = _prep_ffn(f1_w1[l], f1_w3[l], f1_w2[l])
        f2 = _prep_ffn(f2_w1[l], f2_w3[l], f2_w2[l])
        wqkv, bqkv, wuv, buv, wg, bg = _prep_w_in(w_in[l], b_in[l])
        sln = (sgu_ln_g[l][None, :], sgu_ln_b[l][None, :])
        woa, wob, wout = w_oa[l].astype(BF16), w_ob[l].astype(BF16), w_out[l].astype(BF16)
        ln1 = (ln1_g[l][None, :], ln1_b[l][None, :])
        ln2 = (ln2_g[l][None, :], ln2_b[l][None, :])
        ln3 = (ln3_g[l][None, :], ln3_b[l][None, :])

        hp = _ffn_ln(yp, *f1, *ln1, alpha)
        res = _proj_prompt(hp, batch, seq, wqkv, bqkv, wuv, buv, wg, bg, *sln,
                           sgu_ws[l], jnp.transpose(sgu_b[l]))
        qkvs, wins, ybp, sgp = res[0:3], res[3:6], res[6], res[7]
        ols = _attn_prompt(rel_bias, qkvs, batch)
        h2 = _mix(hp, ols, ybp, sgp, woa, wob, wout, *ln2, alpha, batch, seq)
        yp = _ffn_ln(h2, *f2, *ln3, alpha)
        for g in range(N_GROUPS):
            outs[g].append(_to_window(wins[g]))

        hs = _ffn_ln(ys, *f1, *ln1, alpha)
        ws4 = sgu_ws[l][:, :dec_seq, :dec_seq].reshape(SGU_GROUPS, dec_seq * dec_seq)
        sb4 = sgu_b[l][:, :dec_seq]
        qs, kvs, ybs, sgs, vn = _proj_sample(hs, dec_seq, ws4, sb4, wqkv, bqkv, wuv, buv, wg, bg, *sln)
        ols_s = []
        for g in range(N_GROUPS):
            st = states[g][l]
            lg = st.shape[1]
            st_cm = jnp.transpose(st, (0, 2, 3, 4, 1)).reshape(dec_batch, 2 * GW, lg)
            q16 = jnp.tile(qs[:, g * GW:(g + 1) * GW].reshape(dec_batch, 1, dec_seq, GW),
                           (1, GROUP_HEADS, 1, 1)).reshape(dec_batch, GROUP_HEADS * dec_seq, GW)
            kvnew = kvs[:, g * 2 * GW:(g + 1) * 2 * GW].reshape(dec_batch, dec_seq, 2 * GW)
            win_cm, ol = _attn_sample(g, rel_bias, q16, kvnew, st_cm)
            win = jnp.transpose(win_cm.reshape(dec_batch, 2, GROUP_HEADS, HEAD_DIM, lg), (0, 4, 1, 2, 3))
            outs[3 + g].append(win)
            ols_s.append(ol.reshape(1, 1, dec_batch * dec_seq, 2 * GW))
        h2s = _mix(hs, ols_s, ybs, sgs, woa, wob, wout, *ln2, alpha, 1, dec_batch * dec_seq)
        ys = _ffn_ln(h2s, *f2, *ln3, alpha)
        outs[6].append(vn.reshape(dec_batch, dec_seq, SGU_W))

    return (yp.reshape(batch, seq, D_MODEL), ys.reshape(dec_batch, dec_seq, D_MODEL),
            *[jnp.stack(o) for o in outs])
```

```python
import functools
import math

import jax
import jax.numpy as jnp
import numpy as np
from jax import lax
from jax.experimental import pallas as pl
from jax.experimental.pallas import tpu as pltpu

F32 = jnp.float32
BF16 = jnp.bfloat16

D_MODEL = 1024
HEAD_DIM = 64
GROUP_HEADS = 4
ATT_GROUPS = ((128, 1), (512, 4), (2048, 16))
N_GROUPS = len(ATT_GROUPS)
N_ATT_HEADS = N_GROUPS * GROUP_HEADS
ATT_W = N_ATT_HEADS * HEAD_DIM
GW = GROUP_HEADS * HEAD_DIM
BLK = 128
N_BUCKETS = 32
MAX_DISTANCE = 2048
SGU_CHUNK = 128
SGU_GROUPS = 4
SGU_GROUP_W = 128
SGU_W = SGU_GROUPS * SGU_GROUP_W
D_FF = 2816
LN_EPS = 1e-5
NEG = -1e30
SCALE = HEAD_DIM ** -0.5

TM = 512
FC = 256
NC = D_FF // FC
VMEM_LIMIT = 56 * 1024 * 1024
SAMPLE_STEP_BYTES = 2 * 1024 * 1024


def _t5_bucket(dist):
    dist = np.asarray(dist, np.int64)
    max_exact = N_BUCKETS // 2
    large = max_exact + (np.log(np.maximum(dist, max_exact) / max_exact)
                         / np.log(MAX_DISTANCE / max_exact) * (N_BUCKETS - max_exact)).astype(np.int64)
    large = np.minimum(large, N_BUCKETS - 1)
    return np.where(dist < max_exact, dist, large).astype(np.int32)


def _ln(t, g, b):
    mu = jnp.mean(t, axis=-1, keepdims=True)
    d = t - mu
    var = jnp.mean(d * d, axis=-1, keepdims=True)
    return d * lax.rsqrt(var + LN_EPS) * g + b


def _params(n_axes):
    return pltpu.CompilerParams(dimension_semantics=("arbitrary",) * n_axes,
                                vmem_limit_bytes=VMEM_LIMIT)


def _whole(shape):
    nd = len(shape)
    return pl.BlockSpec(shape, lambda *_: (0,) * nd)


def _ffn_ln_kernel(alpha, x_ref, w13_ref, w2_ref, g_ref, b_ref, o_ref, xb_ref, acc_ref):
    xb_ref[...] = x_ref[...].astype(BF16)

    for c in range(NC):
        ab = jnp.dot(xb_ref[...], w13_ref[c], preferred_element_type=F32)
        a = ab[:, :FC]
        b = ab[:, FC:]
        hid = (a * jax.nn.sigmoid(a)) * b
        part = jnp.dot(hid.astype(BF16), w2_ref[c], preferred_element_type=F32)
        if c == 0:
            acc_ref[...] = part
        else:
            acc_ref[...] += part

    t = alpha * x_ref[...] + 0.5 * acc_ref[...]
    o_ref[...] = _ln(t, g_ref[...], b_ref[...])


def _ffn_ln(x2d, w13, w2, g, b, alpha):
    n = x2d.shape[0]
    assert n % TM == 0
    return pl.pallas_call(
        functools.partial(_ffn_ln_kernel, alpha),
        grid=(n // TM,),
        in_specs=[pl.BlockSpec((TM, D_MODEL), lambda i: (i, 0)),
                  _whole((NC, D_MODEL, 2 * FC)), _whole((NC, FC, D_MODEL)),
                  _whole((1, D_MODEL)), _whole((1, D_MODEL))],
        out_specs=pl.BlockSpec((TM, D_MODEL), lambda i: (i, 0)),
        out_shape=jax.ShapeDtypeStruct((n, D_MODEL), F32),
        scratch_shapes=[pltpu.VMEM((TM, D_MODEL), BF16), pltpu.VMEM((TM, D_MODEL), F32)],
        compiler_params=_params(1),
        name="ffn_ln",
    )(x2d, w13, w2, g, b)


def _gelu_exact(z):
    return 0.5 * z * (1.0 + lax.erf(z * (1.0 / math.sqrt(2.0))))


def _uv_branch(hb, wuv_ref, buv_ref, sg_ref, sb_ref):
    zuv = jnp.dot(hb, wuv_ref[...], preferred_element_type=F32) + buv_ref[...]
    zz = _gelu_exact(zuv)
    u = zz[:, :SGU_W]
    vn = _ln(zz[:, SGU_W:], sg_ref[...], sb_ref[...])
    return u, vn


def _gates(hb, wg_ref, bg_ref, sg_out_ref):
    zg = jnp.dot(hb, wg_ref[...], preferred_element_type=F32) + bg_ref[...]
    sg_out_ref[...] = jax.nn.sigmoid(zg).astype(BF16)


def _proj_prompt_kernel(h_ref, wqkv_ref, bqkv_ref, wuv_ref, buv_ref, wg_ref, bg_ref,
                        sln_g_ref, sln_b_ref, ws_ref, sbt_ref,
                        qkv0_ref, qkv1_ref, qkv2_ref, win0_ref, win1_ref, win2_ref,
                        yb_ref, sg_ref, zs_ref):
    hb = h_ref[...].astype(BF16)
    qkv_refs = (qkv0_ref, qkv1_ref, qkv2_ref)
    win_refs = (win0_ref, win1_ref, win2_ref)
    for g, (window, dil) in enumerate(ATT_GROUPS):
        zg = jnp.dot(hb, wqkv_ref[g], preferred_element_type=F32) + bqkv_ref[g]
        keep = min(window, TM)
        win_refs[g][0] = zg[TM - keep:, GW:]
        zq = jnp.concatenate([zg[:, :GW] * SCALE, zg[:, GW:]], axis=1)
        if dil == 1:
            qkv_refs[g][0, 0] = zq.astype(BF16)
            continue
        ncol = 3 * GW // BLK
        for j in range(ncol):
            zs_ref[j] = zq[:, j * BLK:(j + 1) * BLK]
        rows = TM // dil
        for r in range(dil):
            sub = jnp.concatenate([zs_ref[j, pl.ds(r, rows, stride=dil), :] for j in range(ncol)], axis=1)
            qkv_refs[g][0, r] = sub.astype(BF16)

    u, vn = _uv_branch(hb, wuv_ref, buv_ref, sln_g_ref, sln_b_ref)
    vnb = vn.astype(BF16)
    row = lax.broadcasted_iota(jnp.int32, (SGU_CHUNK, SGU_CHUNK), 0)
    col = lax.broadcasted_iota(jnp.int32, (SGU_CHUNK, SGU_CHUNK), 1)
    for g in range(SGU_GROUPS):
        wm = jnp.where(row >= col, ws_ref[g], 0.0).astype(BF16)
        bias = jnp.broadcast_to(sbt_ref[:, g:g + 1], (SGU_CHUNK, SGU_GROUP_W))
        cs = slice(g * SGU_GROUP_W, (g + 1) * SGU_GROUP_W)
        for c in range(TM // SGU_CHUNK):
            rs = slice(c * SGU_CHUNK, (c + 1) * SGU_CHUNK)
            mixed = jnp.dot(wm, vnb[rs, cs], preferred_element_type=F32) + bias
            yb_ref[rs, cs] = (u[rs, cs] * mixed).astype(BF16)

    _gates(hb, wg_ref, bg_ref, sg_ref)


def _proj_prompt(h2d, batch, seq, wqkv, bqkv, wuv, buv, wg, bg, sln_g, sln_b, ws, sbt):
    nt = seq // TM
    assert seq % TM == 0 and TM == ATT_GROUPS[1][0] and TM >= ATT_GROUPS[0][0]
    in_specs = [pl.BlockSpec((TM, D_MODEL), lambda b, t: (b * nt + t, 0)),
                _whole(wqkv.shape), _whole(bqkv.shape), _whole(wuv.shape), _whole(buv.shape),
                _whole(wg.shape), _whole(bg.shape), _whole(sln_g.shape), _whole(sln_b.shape),
                _whole(ws.shape), _whole(sbt.shape)]
    out_shape, out_specs = [], []
    for window, dil in ATT_GROUPS:
        out_shape.append(jax.ShapeDtypeStruct((batch, dil, seq // dil, 3 * GW), BF16))
        out_specs.append(pl.BlockSpec((1, dil, TM // dil, 3 * GW), lambda b, t: (b, 0, t, 0)))
    for window, dil in ATT_GROUPS:
        keep = min(window, seq)
        out_shape.append(jax.ShapeDtypeStruct((batch, keep, 2 * GW), F32))
        if keep > TM:
            assert keep == seq
            out_specs.append(pl.BlockSpec((1, TM, 2 * GW), lambda b, t: (b, t, 0)))
        else:
            out_specs.append(pl.BlockSpec((1, keep, 2 * GW), lambda b, t: (b, 0, 0)))
    out_shape += [jax.ShapeDtypeStruct((batch * seq, SGU_W), BF16),
                  jax.ShapeDtypeStruct((batch * seq, 2 * D_MODEL), BF16)]
    out_specs += [pl.BlockSpec((TM, SGU_W), lambda b, t: (b * nt + t, 0)),
                  pl.BlockSpec((TM, 2 * D_MODEL), lambda b, t: (b * nt + t, 0))]
    return pl.pallas_call(
        _proj_prompt_kernel,
        grid=(batch, nt),
        in_specs=in_specs,
        out_specs=out_specs,
        out_shape=out_shape,
        scratch_shapes=[pltpu.VMEM((3 * GW // BLK, TM, BLK), F32)],
        compiler_params=_params(2),
        name="proj_prompt",
    )(h2d, wqkv, bqkv, wuv, buv, wg, bg, sln_g, sln_b, ws, sbt)


def _proj_sample_kernel(t_len, ws4_ref, sb4_ref, h_ref, wqkv_ref, bqkv_ref, wuv_ref, buv_ref,
                        wg_ref, bg_ref, sln_g_ref, sln_b_ref,
                        q_ref, kv_ref, yb_ref, sg_ref, vn_ref):
    n = h_ref.shape[0]
    hb = h_ref[...].astype(BF16)
    for g in range(N_GROUPS):
        zg = jnp.dot(hb, wqkv_ref[g], preferred_element_type=F32) + bqkv_ref[g]
        q_ref[:, g * GW:(g + 1) * GW] = zg[:, :GW] * SCALE
        kv_ref[:, g * 2 * GW:(g + 1) * 2 * GW] = zg[:, GW:]

    u, vn = _uv_branch(hb, wuv_ref, buv_ref, sln_g_ref, sln_b_ref)
    vn_ref[...] = vn
    p = lax.broadcasted_iota(jnp.int32, (n, SGU_GROUP_W), 0) & (t_len - 1)
    for g in range(SGU_GROUPS):
        cs = slice(g * SGU_GROUP_W, (g + 1) * SGU_GROUP_W)
        vg = vn[:, cs]
        acc = jnp.zeros((n, SGU_GROUP_W), F32)
        for pp in range(t_len):
            acc = jnp.where(p == pp, sb4_ref[g, pp], acc)
        for d in range(t_len):
            coef = jnp.zeros((n, SGU_GROUP_W), F32)
            for pp in range(d, t_len):
                coef = jnp.where(p == pp, ws4_ref[g, pp * t_len + (pp - d)], coef)
            shifted = vg if d == 0 else pltpu.roll(vg, d, axis=0)
            acc = acc + coef * shifted
        yb_ref[:, cs] = (u[:, cs] * acc).astype(BF16)

    _gates(hb, wg_ref, bg_ref, sg_ref)


def _proj_sample(h2d, t_len, ws4, sb4, wqkv, bqkv, wuv, buv, wg, bg, sln_g, sln_b):
    n = h2d.shape[0]
    assert t_len & (t_len - 1) == 0
    smem = pl.BlockSpec(memory_space=pltpu.SMEM)
    ins = (h2d, wqkv, bqkv, wuv, buv, wg, bg, sln_g, sln_b)
    out_shape = [jax.ShapeDtypeStruct((n, N_GROUPS * GW), F32),
                 jax.ShapeDtypeStruct((n, N_GROUPS * 2 * GW), F32),
                 jax.ShapeDtypeStruct((n, SGU_W), BF16),
                 jax.ShapeDtypeStruct((n, 2 * D_MODEL), BF16),
                 jax.ShapeDtypeStruct((n, SGU_W), F32)]
    return pl.pallas_call(
        functools.partial(_proj_sample_kernel, t_len),
        grid=(1,),
        in_specs=[smem, smem] + [_whole(a.shape) for a in ins],
        out_specs=[_whole(s.shape) for s in out_shape],
        out_shape=out_shape,
        compiler_params=_params(1),
        name="proj_sample",
    )(ws4, sb4, *ins)


def _bias_mask(bmap, buckets, value_of_bucket):
    acc = jnp.full(bmap.shape, NEG, F32)
    for bkt in buckets:
        acc = jnp.where(bmap == bkt, value_of_bucket(bkt), acc)
    return acc


def _prompt_bucket_map(window, dil):
    band = window // dil
    rel = np.arange(BLK)[:, None] - np.arange(2 * BLK)[None, :] + BLK
    valid = (rel >= 0) & (rel <= band)
    return np.where(valid, _t5_bucket(np.clip(rel, 0, band) * dil), -1).astype(np.int32)


def _attn_prompt_kernel(buckets, tab_ref, bmap_ref, qkv0_ref, qkv1_ref, qkv2_ref,
                        ol0_ref, ol1_ref, ol2_ref, bm_ref):
    @pl.when(pl.program_id(0) == 0)
    def _():
        for g in range(N_GROUPS):
            bmap = bmap_ref[g]
            for h in range(GROUP_HEADS):
                rs = slice(h * BLK, (h + 1) * BLK)
                bm = _bias_mask(bmap, buckets[g], lambda bkt: tab_ref[bkt, g * GROUP_HEADS + h])
                bm_ref[2 * g, rs, :] = bm
                bm_ref[2 * g + 1, rs, :BLK] = bm[:, BLK:]
                bm_ref[2 * g + 1, rs, BLK:] = jnp.full((BLK, BLK), NEG, F32)

    lane_head = lax.broadcasted_iota(jnp.int32, (BLK, GW), 1) >> 6
    lane_lo = lax.broadcasted_iota(jnp.int32, (BLK, BLK), 1) < HEAD_DIM

    def unit(g, qkv_ref, ol_ref, r, n):
        single = qkv_ref.shape[2] == BLK
        static_n = isinstance(n, int)
        q0 = n * BLK if static_n else pl.multiple_of(n * BLK, BLK)
        q = qkv_ref[0, r, pl.ds(q0, BLK), 0:GW]
        qs = jnp.concatenate([jnp.where(lane_head == h, q, jnp.zeros_like(q))
                              for h in range(GROUP_HEADS)], axis=0)
        if single:
            k = qkv_ref[0, r, :, GW:2 * GW]
            v = qkv_ref[0, r, :, 2 * GW:3 * GW]
            bm = bm_ref[2 * g, :, BLK:]
        else:
            if static_n:
                k0 = max(q0 - BLK, 0)
                bm = bm_ref[2 * g + int(n == 0)]
            else:
                k0 = pl.multiple_of(jnp.maximum(q0 - BLK, 0), BLK)
                bm = bm_ref[2 * g + (n == 0).astype(jnp.int32)]
            k = qkv_ref[0, r, pl.ds(k0, 2 * BLK), GW:2 * GW]
            v = qkv_ref[0, r, pl.ds(k0, 2 * BLK), 2 * GW:3 * GW]
        s = lax.dot_general(qs, k, (((1,), (1,)), ((), ())), preferred_element_type=F32) + bm
        m = jnp.max(s, axis=-1, keepdims=True)
        p = jnp.exp(s - m)
        l = jnp.sum(p, axis=-1, keepdims=True)
        pb = p.astype(BF16)
        linv = 1.0 / l
        lse = m + jnp.log(l)
        rows = pl.ds(q0, BLK)
        for half in range(2):
            pv = jnp.dot(pb[2 * half * BLK:(2 * half + 2) * BLK], v[:, half * BLK:(half + 1) * BLK],
                         preferred_element_type=F32)
            ra = slice(2 * half * BLK, (2 * half + 1) * BLK)
            rb = slice((2 * half + 1) * BLK, (2 * half + 2) * BLK)
            o = jnp.where(lane_lo, pv[:BLK] * linv[ra], pv[BLK:] * linv[rb])
            ls = jnp.where(lane_lo, lse[ra], lse[rb])
            ol_ref[0, r, rows, half * BLK:(half + 1) * BLK] = o
            ol_ref[0, r, rows, GW + half * BLK:GW + (half + 1) * BLK] = ls

    unroll = 4
    for g, (qkv_ref, ol_ref) in enumerate(((qkv0_ref, ol0_ref), (qkv1_ref, ol1_ref), (qkv2_ref, ol2_ref))):
        dil, sub_len = qkv_ref.shape[1], qkv_ref.shape[2]
        nb = sub_len // BLK
        n_units = dil * nb
        assert n_units % unroll == 0 and (nb == 1 or nb == unroll or dil == 1)

        def trip(t, carry, g=g, qkv_ref=qkv_ref, ol_ref=ol_ref, nb=nb, dil=dil):
            for j in range(unroll):
                if nb == 1:
                    r, n = t * unroll + j, 0
                elif dil == 1:
                    r, n = 0, t * unroll + j
                else:
                    r, n = t, j
                unit(g, qkv_ref, ol_ref, r, n)
            return carry

        lax.fori_loop(0, n_units // unroll, trip, 0)


def _attn_prompt(tab, qkvs, batch):
    bmaps = [_prompt_bucket_map(w, d) for w, d in ATT_GROUPS]
    buckets = tuple(tuple(int(b) for b in np.unique(m) if b >= 0) for m in bmaps)
    bmap = jnp.asarray(np.stack(bmaps))
    in_specs = [pl.BlockSpec(memory_space=pltpu.SMEM), _whole(bmap.shape)]
    out_shape, out_specs = [], []
    for a in qkvs:
        _, dil, sub_len, _ = a.shape
        in_specs.append(pl.BlockSpec((1, dil, sub_len, 3 * GW), lambda b: (b, 0, 0, 0)))
        out_shape.append(jax.ShapeDtypeStruct((batch, dil, sub_len, 2 * GW), F32))
        out_specs.append(pl.BlockSpec((1, dil, sub_len, 2 * GW), lambda b: (b, 0, 0, 0)))
    return pl.pallas_call(
        functools.partial(_attn_prompt_kernel, buckets),
        grid=(batch,),
        in_specs=in_specs,
        out_specs=out_specs,
        out_shape=out_shape,
        scratch_shapes=[pltpu.VMEM((2 * N_GROUPS, GROUP_HEADS * BLK, 2 * BLK), F32)],
        compiler_params=_params(1),
        name="attn_prompt",
    )(tab, bmap, *qkvs)


def _attn_sample_kernel(g, t_len, buckets, tab_ref, bmap_ref, q_ref, kvnew_ref, st_ref,
                        win_ref, ol_ref, bm_ref, tr_ref):
    lg = st_ref.shape[2]
    nrow = GROUP_HEADS * t_len
    row_head = lax.broadcasted_iota(jnp.int32, (nrow, GW), 0) // t_len
    lane_head = lax.broadcasted_iota(jnp.int32, (nrow, GW), 1) >> 6

    @pl.when(pl.program_id(0) == 0)
    def _():
        rh = lax.broadcasted_iota(jnp.int32, bm_ref.shape, 0) // t_len

        def value(bkt):
            v = jnp.full(bm_ref.shape, tab_ref[bkt, g * GROUP_HEADS], F32)
            for h in range(1, GROUP_HEADS):
                v = jnp.where(rh == h, tab_ref[bkt, g * GROUP_HEADS + h], v)
            return v

        bm_ref[...] = _bias_mask(bmap_ref[...], buckets, value)
        tr_ref[...] = jnp.zeros_like(tr_ref)

    ext = lg + BLK
    sel = row_head == lane_head
    for i in range(st_ref.shape[0]):
        tr_ref[i, 0:t_len, :] = kvnew_ref[i]
        tail = jnp.concatenate([tr_ref[i, :, j * BLK:(j + 1) * BLK].T for j in range(2 * GW // BLK)], axis=0)
        full = jnp.concatenate([st_ref[i], tail], axis=1)
        win_ref[i] = pltpu.roll(full, ext - t_len, axis=1)[:, :lg]

        fb = full.astype(BF16)
        qs = jnp.where(sel, q_ref[i], 0.0).astype(BF16)
        s = jnp.dot(qs, fb[0:GW], preferred_element_type=F32) + bm_ref[...]
        m = jnp.max(s, axis=-1, keepdims=True)
        p = jnp.exp(s - m)
        l = jnp.sum(p, axis=-1, keepdims=True)
        pv = lax.dot_general(p.astype(BF16), fb[GW:2 * GW], (((1,), (1,)), ((), ())),
                             preferred_element_type=F32) * (1.0 / l)
        lse = jnp.broadcast_to(m + jnp.log(l), (nrow, GW))
        o = jnp.where(sel, pv, 0.0)
        ls = jnp.where(sel, lse, 0.0)
        osum, lsum = o, ls
        for h in range(1, GROUP_HEADS):
            osum = osum + pltpu.roll(o, h * t_len, axis=0)
            lsum = lsum + pltpu.roll(ls, h * t_len, axis=0)
        ol_ref[i, :, 0:GW] = osum[0:t_len]
        ol_ref[i, :, GW:2 * GW] = lsum[0:t_len]


def _sample_bucket_map(window, dil, lg, t_len):
    band = window // dil
    ext = lg + BLK
    bmap = np.full((t_len, ext), -1, np.int64)
    t = np.arange(t_len)[:, None]
    j = np.arange(lg + t_len)[None, :]
    dist = lg + t - j
    valid = (dist >= 0) & (dist % dil == 0) & (dist <= band * dil)
    bmap[:, :lg + t_len] = np.where(valid, _t5_bucket(np.maximum(dist, 0)), -1)
    return np.tile(bmap, (GROUP_HEADS, 1)).astype(np.int32)


def _attn_sample(g, tab, q16, kvnew, st):
    window, dil = ATT_GROUPS[g]
    nb, _, lg = st.shape
    t_len = kvnew.shape[1]
    nrow = GROUP_HEADS * t_len
    bmap_np = _sample_bucket_map(window, dil, lg, t_len)
    buckets = tuple(int(b) for b in np.unique(bmap_np) if b >= 0)
    bmap = jnp.asarray(bmap_np)
    per = max(1, min(nb, SAMPLE_STEP_BYTES // (2 * GW * lg * 4)))
    assert nb % per == 0
    return pl.pallas_call(
        functools.partial(_attn_sample_kernel, g, t_len, buckets),
        grid=(nb // per,),
        in_specs=[pl.BlockSpec(memory_space=pltpu.SMEM), _whole(bmap.shape),
                  pl.BlockSpec((per, nrow, GW), lambda b: (b, 0, 0)),
                  pl.BlockSpec((per, t_len, 2 * GW), lambda b: (b, 0, 0)),
                  pl.BlockSpec((per, 2 * GW, lg), lambda b: (b, 0, 0))],
        out_specs=[pl.BlockSpec((per, 2 * GW, lg), lambda b: (b, 0, 0)),
                   pl.BlockSpec((per, t_len, 2 * GW), lambda b: (b, 0, 0))],
        out_shape=[jax.ShapeDtypeStruct((nb, 2 * GW, lg), F32),
                   jax.ShapeDtypeStruct((nb, t_len, 2 * GW), F32)],
        scratch_shapes=[pltpu.VMEM((nrow, lg + BLK), F32), pltpu.VMEM((per, BLK, 2 * GW), F32)],
        compiler_params=_params(1),
        name="attn_sample_g%d" % g,
    )(tab, bmap, q16, kvnew, st)


def _mix_kernel(alpha, h_ref, ol0_ref, ol1_ref, ol2_ref, yb_ref, sg_ref, woa_ref, wob_ref, wout_ref,
                g_ref, b_ref, o_ref, u_ref):
    tm = h_ref.shape[0]
    ncol = 2 * GW // BLK
    outs, lses = [], []
    for gi, ol_ref in enumerate((ol0_ref, ol1_ref, ol2_ref)):
        dil = ol_ref.shape[1]
        if dil == 1:
            ol = ol_ref[0, 0]
        else:
            for r in range(dil):
                for j in range(ncol):
                    u_ref[gi, j, pl.ds(r, tm // dil, stride=dil), :] = ol_ref[0, r, :, j * BLK:(j + 1) * BLK]
            ol = jnp.concatenate([u_ref[gi, j] for j in range(ncol)], axis=1)
        outs.append(ol[:, :GW])
        lses.append(ol[:, GW:])
    mx = jnp.maximum(jnp.maximum(lses[0], lses[1]), lses[2])
    es = [jnp.exp(ls - mx) for ls in lses]
    den = es[0] + es[1] + es[2]
    num = es[0] * outs[0] + es[1] * outs[1] + es[2] * outs[2]
    att = (num * (1.0 / den)).astype(BF16)
    a = jnp.dot(att, woa_ref[...], preferred_element_type=F32)
    bb = jnp.dot(yb_ref[...], wob_ref[...], preferred_element_type=F32)
    gated = sg_ref[:, :D_MODEL].astype(F32) * a + sg_ref[:, D_MODEL:].astype(F32) * bb
    mix = jnp.dot(gated.astype(BF16), wout_ref[...], preferred_element_type=F32)
    o_ref[...] = _ln(alpha * h_ref[...] + mix, g_ref[...], b_ref[...])


def _mix(h2d, ols, yb, sg, woa, wob, wout, g, b, alpha, batch, seq):
    tm = min(TM, seq)
    nt = seq // tm
    in_specs = [pl.BlockSpec((tm, D_MODEL), lambda bi, t: (bi * nt + t, 0))]
    for a in ols:
        dil = a.shape[1]
        in_specs.append(pl.BlockSpec((1, dil, tm // dil, 2 * GW), lambda bi, t: (bi, 0, t, 0)))
    in_specs += [pl.BlockSpec((tm, SGU_W), lambda bi, t: (bi * nt + t, 0)),
                 pl.BlockSpec((tm, 2 * D_MODEL), lambda bi, t: (bi * nt + t, 0)),
                 _whole(woa.shape), _whole(wob.shape), _whole(wout.shape),
                 _whole(g.shape), _whole(b.shape)]
    return pl.pallas_call(
        functools.partial(_mix_kernel, alpha),
        grid=(batch, nt),
        in_specs=in_specs,
        out_specs=pl.BlockSpec((tm, D_MODEL), lambda bi, t: (bi * nt + t, 0)),
        out_shape=jax.ShapeDtypeStruct((batch * seq, D_MODEL), F32),
        scratch_shapes=[pltpu.VMEM((N_GROUPS, 2 * GW // BLK, tm, BLK), F32)],
        compiler_params=_params(2),
        name="mix",
    )(h2d, *ols, yb, sg, woa, wob, wout, g, b)


def _prep_ffn(w1, w3, w2):
    w13 = jnp.concatenate([w1.reshape(D_MODEL, NC, FC), w3.reshape(D_MODEL, NC, FC)], axis=-1)
    return (jnp.transpose(w13, (1, 0, 2)).astype(BF16), w2.reshape(NC, FC, D_MODEL).astype(BF16))


def _prep_w_in(w_in, b_in):
    def cols(a, lo, n):
        return a[..., lo:lo + n]
    wq, bq = [], []
    for g in range(N_GROUPS):
        parts = [(i * ATT_W + g * GW, GW) for i in range(3)]
        wq.append(jnp.concatenate([cols(w_in, lo, n) for lo, n in parts], axis=-1))
        bq.append(jnp.concatenate([cols(b_in, lo, n) for lo, n in parts], axis=-1))
    o_uv = 3 * ATT_W
    return (jnp.stack(wq).astype(BF16), jnp.stack(bq)[:, None, :],
            cols(w_in, o_uv, 2 * SGU_W).astype(BF16), cols(b_in, o_uv, 2 * SGU_W)[None, :],
            cols(w_in, o_uv + 2 * SGU_W, 2 * D_MODEL).astype(BF16),
            cols(b_in, o_uv + 2 * SGU_W, 2 * D_MODEL)[None, :])


def _to_window(a):
    return a.reshape(a.shape[0], a.shape[1], 2, GROUP_HEADS, HEAD_DIM)


def kernel(x_prompt, x_sample, state_win0, state_win1, state_win2, rel_bias, ln1_g, ln1_b, f1_w1, f1_w3, f1_w2, w_in, b_in, sgu_ln_g, sgu_ln_b, sgu_ws, sgu_b, w_oa, w_ob, w_out, ln2_g, ln2_b, f2_w1, f2_w3, f2_w2, ln3_g, ln3_b):
    depth = ln1_g.shape[0]
    alpha = (2 * depth) ** 0.25
    batch, seq, _ = x_prompt.shape
    dec_batch, dec_seq, _ = x_sample.shape
    states = (state_win0, state_win1, state_win2)

    yp = x_prompt.reshape(batch * seq, D_MODEL)
    ys = x_sample.reshape(dec_batch * dec_seq, D_MODEL)
    outs = [[] for _ in range(7)]
    for l in range(depth):
        f1 = _prep_ffn(f1_w1[l], f1_w3[l], f1_w2[l])
        f2 = _prep_ffn(f2_w1[l], f2_w3[l], f2_w2[l])
        wqkv, bqkv, wuv, buv, wg, bg = _prep_w_in(w_in[l], b_in[l])
        sln = (sgu_ln_g[l][None, :], sgu_ln_b[l][None, :])
        woa, wob, wout = w_oa[l].astype(BF16), w_ob[l].astype(BF16), w_out[l].astype(BF16)
        ln1 = (ln1_g[l][None, :], ln1_b[l][None, :])
        ln2 = (ln2_g[l][None, :], ln2_b[l][None, :])
        ln3 = (ln3_g[l][None, :], ln3_b[l][None, :])

        hp = _ffn_ln(yp, *f1, *ln1, alpha)
        res = _proj_prompt(hp, batch, seq, wqkv, bqkv, wuv, buv, wg, bg, *sln,
                           sgu_ws[l], jnp.transpose(sgu_b[l]))
        qkvs, wins, ybp, sgp = res[0:3], res[3:6], res[6], res[7]
        ols = _attn_prompt(rel_bias, qkvs, batch)
        h2 = _mix(hp, ols, ybp, sgp, woa, wob, wout, *ln2, alpha, batch, seq)
        yp = _ffn_ln(h2, *f2, *ln3, alpha)
        for g in range(N_GROUPS):
            outs[g].append(_to_window(wins[g]))

        hs = _ffn_ln(ys, *f1, *ln1, alpha)
        ws4 = sgu_ws[l][:, :dec_seq, :dec_seq].reshape(SGU_GROUPS, dec_seq * dec_seq)
        sb4 = sgu_b[l][:, :dec_seq]
        qs, kvs, ybs, sgs, vn = _proj_sample(hs, dec_seq, ws4, sb4, wqkv, bqkv, wuv, buv, wg, bg, *sln)
        ols_s = []
        for g in range(N_GROUPS):
            st = states[g][l]
            lg = st.shape[1]
            st_cm = jnp.transpose(st, (0, 2, 3, 4, 1)).reshape(dec_batch, 2 * GW, lg)
            q16 = jnp.tile(qs[:, g * GW:(g + 1) * GW].reshape(dec_batch, 1, dec_seq, GW),
                           (1, GROUP_HEADS, 1, 1)).reshape(dec_batch, GROUP_HEADS * dec_seq, GW)
            kvnew = kvs[:, g * 2 * GW:(g + 1) * 2 * GW].reshape(dec_batch, dec_seq, 2 * GW)
            win_cm, ol = _attn_sample(g, rel_bias, q16, kvnew, st_cm)
            win = jnp.transpose(win_cm.reshape(dec_batch, 2, GROUP_HEADS, HEAD_DIM, lg), (0, 4, 1, 2, 3))
            outs[3 + g].append(win)
            ols_s.append(ol.reshape(1, 1, dec_batch * dec_seq, 2 * GW))
        h2s = _mix(hs, ols_s, ybs, sgs, woa, wob, wout, *ln2, alpha, 1, dec_batch * dec_seq)
        ys = _ffn_ln(h2s, *f2, *ln3, alpha)
        outs[6].append(vn.reshape(dec_batch, dec_seq, SGU_W))

    return (yp.reshape(batch, seq, D_MODEL), ys.reshape(dec_batch, dec_seq, D_MODEL),
            *[jnp.stack(o) for o in outs])
```

```python
import functools
import math
from typing import Callable, NamedTuple

import jax
import jax.numpy as jnp
import numpy as np
from jax import lax
from jax.experimental import pallas as pl
from jax.experimental.pallas import tpu as pltpu

F32 = jnp.float32
BF16 = jnp.bfloat16

D_MODEL = 1024
HEAD_DIM = 64
GROUP_HEADS = 4
ATT_GROUPS = ((128, 1), (512, 4), (2048, 16))
N_GROUPS = len(ATT_GROUPS)
N_ATT_HEADS = N_GROUPS * GROUP_HEADS
ATT_W = N_ATT_HEADS * HEAD_DIM
GW = GROUP_HEADS * HEAD_DIM
BLK = 128
N_BUCKETS = 32
MAX_DISTANCE = 2048
SGU_CHUNK = 128
SGU_GROUPS = 4
SGU_GROUP_W = 128
SGU_W = SGU_GROUPS * SGU_GROUP_W
D_FF = 2816
LN_EPS = 1e-5
NEG = -1e30
SCALE = HEAD_DIM ** -0.5

TM = 512
FC = 256
NC = D_FF // FC
MIX_CHUNKS = 2
VMEM_LIMIT = 60 * 1024 * 1024


def _t5_bucket(dist):
    dist = np.asarray(dist, np.int64)
    max_exact = N_BUCKETS // 2
    large = max_exact + (np.log(np.maximum(dist, max_exact) / max_exact)
                         / np.log(MAX_DISTANCE / max_exact) * (N_BUCKETS - max_exact)).astype(np.int64)
    large = np.minimum(large, N_BUCKETS - 1)
    return np.where(dist < max_exact, dist, large).astype(np.int32)


def _ln(t, g, b):
    mu = jnp.mean(t, axis=-1, keepdims=True)
    d = t - mu
    var = jnp.mean(d * d, axis=-1, keepdims=True)
    return d * lax.rsqrt(var + LN_EPS) * g + b


def _whole(shape):
    nd = len(shape)
    return pl.BlockSpec(shape, lambda *_: (0,) * nd, pipeline_mode=pl.Buffered(1))


class _Side(NamedTuple):
    name: str
    init: Callable
    step: Callable
    inputs: tuple
    in_specs: tuple
    out_shape: tuple
    out_specs: tuple
    scratch_shapes: tuple


def _take(refs, pos, count):
    return refs[pos:pos + count], pos + count


def _call(name, main, main_init, grid, inputs, in_specs, out_shape, out_specs, scratch, sides=()):
    n_in, n_out, n_scr = len(inputs), len(out_shape), len(scratch)

    def body(*refs):
        refs = list(refs)
        m_in, pos = _take(refs, 0, n_in)
        s_in = []
        for s in sides:
            r, pos = _take(refs, pos, len(s.inputs))
            s_in.append(r)
        m_out, pos = _take(refs, pos, n_out)
        s_out = []
        for s in sides:
            r, pos = _take(refs, pos, len(s.out_shape))
            s_out.append(r)
        m_scr, pos = _take(refs, pos, n_scr)
        s_scr = []
        for s in sides:
            r, pos = _take(refs, pos, len(s.scratch_shapes))
            s_scr.append(r)

        if main_init is not None or sides:
            first = pl.program_id(0) == 0
            for axis in range(1, len(grid)):
                first = jnp.logical_and(first, pl.program_id(axis) == 0)

            @pl.when(first)
            def _():
                if main_init is not None:
                    main_init(m_in, m_out, m_scr)
                for s, i, o, c in zip(sides, s_in, s_out, s_scr):
                    s.init(i, o, c)

        main(m_in, m_out, m_scr)
        for s, i, o, c in zip(sides, s_in, s_out, s_scr):
            s.step(i, o, c)

    all_in = list(inputs) + [a for s in sides for a in s.inputs]
    all_in_specs = list(in_specs) + [a for s in sides for a in s.in_specs]
    all_out_shape = list(out_shape) + [a for s in sides for a in s.out_shape]
    all_out_specs = list(out_specs) + [a for s in sides for a in s.out_specs]
    all_scratch = list(scratch) + [a for s in sides for a in s.scratch_shapes]
    res = pl.pallas_call(
        body,
        grid=grid,
        in_specs=all_in_specs,
        out_specs=all_out_specs,
        out_shape=all_out_shape,
        scratch_shapes=all_scratch,
        compiler_params=pltpu.CompilerParams(dimension_semantics=("arbitrary",) * len(grid),
                                             vmem_limit_bytes=VMEM_LIMIT),
        name=name + "".join("_" + s.name for s in sides),
    )(*all_in)
    res = list(res)
    main_res, pos = _take(res, 0, n_out)
    side_res = []
    for s in sides:
        r, pos = _take(res, pos, len(s.out_shape))
        side_res.append(r)
    return main_res, side_res


def _swiglu_ln(alpha, x_ref, xb_ref, w13_ref, w2_ref, g_ref, b_ref, acc_ref, o_ref):
    for c in range(NC):
        ab = jnp.dot(xb_ref[...], w13_ref[c], preferred_element_type=F32)
        a = ab[:, :FC]
        b = ab[:, FC:]
        hid = (a * jax.nn.sigmoid(a)) * b
        part = jnp.dot(hid.astype(BF16), w2_ref[c], preferred_element_type=F32)
        if c == 0:
            acc_ref[...] = part
        else:
            acc_ref[...] += part
    t = alpha * x_ref[...] + 0.5 * acc_ref[...]
    o_ref[...] = _ln(t, g_ref[...], b_ref[...])


def _ffn_specs():
    return [_whole((NC, D_MODEL, 2 * FC)), _whole((NC, FC, D_MODEL)), _whole((1, D_MODEL)), _whole((1, D_MODEL))]


def _ffn_ln_main(alpha, in_refs, out_refs, scratch_refs):
    x_ref, w13_ref, w2_ref, g_ref, b_ref = in_refs
    xb_ref, acc_ref = scratch_refs
    xb_ref[...] = x_ref[...].astype(BF16)
    _swiglu_ln(alpha, x_ref, xb_ref, w13_ref, w2_ref, g_ref, b_ref, acc_ref, out_refs[0])


def _ffn_ln(x2d, w13, w2, g, b, alpha, sides=()):
    n = x2d.shape[0]
    assert n % TM == 0
    (out,), side_res = _call(
        "ffn_ln", functools.partial(_ffn_ln_main, alpha), None, (n // TM,),
        [x2d, w13, w2, g, b],
        [pl.BlockSpec((TM, D_MODEL), lambda i: (i, 0))] + _ffn_specs(),
        [jax.ShapeDtypeStruct((n, D_MODEL), F32)],
        [pl.BlockSpec((TM, D_MODEL), lambda i: (i, 0))],
        [pltpu.VMEM((TM, D_MODEL), BF16), pltpu.VMEM((TM, D_MODEL), F32)],
        sides)
    return out, side_res


def _gelu_exact(z):
    return 0.5 * z * (1.0 + lax.erf(z * (1.0 / math.sqrt(2.0))))


def _uv_branch(hb, wuv_ref, buv_ref, sg_ref, sb_ref):
    zuv = jnp.dot(hb, wuv_ref[...], preferred_element_type=F32) + buv_ref[...]
    zz = _gelu_exact(zuv)
    u = zz[:, :SGU_W]
    vn = _ln(zz[:, SGU_W:], sg_ref[...], sb_ref[...])
    return u, vn


def _gates(hb, wg_ref, bg_ref, sg_out_ref):
    zg = jnp.dot(hb, wg_ref[...], preferred_element_type=F32) + bg_ref[...]
    sg_out_ref[...] = jax.nn.sigmoid(zg).astype(BF16)


def _proj_prompt_main(in_refs, out_refs, scratch_refs):
    (h_ref, wqkv_ref, bqkv_ref, wuv_ref, buv_ref, wg_ref, bg_ref,
     sln_g_ref, sln_b_ref, ws_ref, sbt_ref) = in_refs
    qkv_refs, win_refs = out_refs[0:3], out_refs[3:6]
    yb_ref, sg_ref = out_refs[6:8]
    (zs_ref,) = scratch_refs
    hb = h_ref[...].astype(BF16)
    u, vn = _uv_branch(hb, wuv_ref, buv_ref, sln_g_ref, sln_b_ref)
    _gates(hb, wg_ref, bg_ref, sg_ref)
    for g, (window, dil) in enumerate(ATT_GROUPS):
        zg = jnp.dot(hb, wqkv_ref[g], preferred_element_type=F32) + bqkv_ref[g]
        keep = min(window, TM)
        win_refs[g][0] = zg[TM - keep:, GW:]
        zq = jnp.concatenate([zg[:, :GW] * SCALE, zg[:, GW:]], axis=1)
        if dil == 1:
            qkv_refs[g][0, 0] = zq.astype(BF16)
            continue
        ncol = 3 * GW // BLK
        for j in range(ncol):
            zs_ref[j] = zq[:, j * BLK:(j + 1) * BLK]
        rows = TM // dil
        for r in range(dil):
            sub = jnp.concatenate([zs_ref[j, pl.ds(r, rows, stride=dil), :] for j in range(ncol)], axis=1)
            qkv_refs[g][0, r] = sub.astype(BF16)

    vnb = vn.astype(BF16)
    row = lax.broadcasted_iota(jnp.int32, (SGU_CHUNK, SGU_CHUNK), 0)
    col = lax.broadcasted_iota(jnp.int32, (SGU_CHUNK, SGU_CHUNK), 1)
    for g in range(SGU_GROUPS):
        wm = jnp.where(row >= col, ws_ref[g], 0.0).astype(BF16)
        bias = jnp.broadcast_to(sbt_ref[:, g:g + 1], (SGU_CHUNK, SGU_GROUP_W))
        cs = slice(g * SGU_GROUP_W, (g + 1) * SGU_GROUP_W)
        for c in range(TM // SGU_CHUNK):
            rs = slice(c * SGU_CHUNK, (c + 1) * SGU_CHUNK)
            mixed = jnp.dot(wm, vnb[rs, cs], preferred_element_type=F32) + bias
            yb_ref[rs, cs] = (u[rs, cs] * mixed).astype(BF16)


def _proj_prompt(h2d, batch, seq, wqkv, bqkv, wuv, buv, wg, bg, sln_g, sln_b, ws, sbt, sides=()):
    nt = seq // TM
    assert seq % TM == 0 and TM == ATT_GROUPS[1][0] and TM >= ATT_GROUPS[0][0]
    ins = [h2d, wqkv, bqkv, wuv, buv, wg, bg, sln_g, sln_b, ws, sbt]
    in_specs = [pl.BlockSpec((TM, D_MODEL), lambda b, t: (b * nt + t, 0))] + [_whole(a.shape) for a in ins[1:]]
    out_shape, out_specs = [], []
    for window, dil in ATT_GROUPS:
        out_shape.append(jax.ShapeDtypeStruct((batch, dil, seq // dil, 3 * GW), BF16))
        out_specs.append(pl.BlockSpec((1, dil, TM // dil, 3 * GW), lambda b, t: (b, 0, t, 0)))
    for window, dil in ATT_GROUPS:
        keep = min(window, seq)
        out_shape.append(jax.ShapeDtypeStruct((batch, keep, 2 * GW), F32))
        if keep > TM:
            assert keep == seq
            out_specs.append(pl.BlockSpec((1, TM, 2 * GW), lambda b, t: (b, t, 0)))
        else:
            out_specs.append(pl.BlockSpec((1, keep, 2 * GW), lambda b, t: (b, 0, 0)))
    out_shape += [jax.ShapeDtypeStruct((batch * seq, SGU_W), BF16),
                  jax.ShapeDtypeStruct((batch * seq, 2 * D_MODEL), BF16)]
    out_specs += [pl.BlockSpec((TM, SGU_W), lambda b, t: (b * nt + t, 0)),
                  pl.BlockSpec((TM, 2 * D_MODEL), lambda b, t: (b * nt + t, 0))]
    return _call("proj_prompt", _proj_prompt_main, None, (batch, nt), ins, in_specs, out_shape, out_specs,
                 [pltpu.VMEM((3 * GW // BLK, TM, BLK), F32)], sides)


def _proj_sample_kernel(t_len, ws4_ref, sb4_ref, h_ref, wqkv_ref, bqkv_ref, wuv_ref, buv_ref,
                        wg_ref, bg_ref, sln_g_ref, sln_b_ref,
                        q_ref, kv_ref, yb_ref, sg_ref, vn_ref):
    n = h_ref.shape[0]
    hb = h_ref[...].astype(BF16)
    for g in range(N_GROUPS):
        zg = jnp.dot(hb, wqkv_ref[g], preferred_element_type=F32) + bqkv_ref[g]
        q_ref[:, g * GW:(g + 1) * GW] = zg[:, :GW] * SCALE
        kv_ref[:, g * 2 * GW:(g + 1) * 2 * GW] = zg[:, GW:]

    u, vn = _uv_branch(hb, wuv_ref, buv_ref, sln_g_ref, sln_b_ref)
    vn_ref[...] = vn
    p = lax.broadcasted_iota(jnp.int32, (n, SGU_GROUP_W), 0) & (t_len - 1)
    for g in range(SGU_GROUPS):
        cs = slice(g * SGU_GROUP_W, (g + 1) * SGU_GROUP_W)
        vg = vn[:, cs]
        acc = jnp.zeros((n, SGU_GROUP_W), F32)
        for pp in range(t_len):
            acc = jnp.where(p == pp, sb4_ref[g, pp], acc)
        for d in range(t_len):
            coef = jnp.zeros((n, SGU_GROUP_W), F32)
            for pp in range(d, t_len):
                coef = jnp.where(p == pp, ws4_ref[g, pp * t_len + (pp - d)], coef)
            shifted = vg if d == 0 else pltpu.roll(vg, d, axis=0)
            acc = acc + coef * shifted
        yb_ref[:, cs] = (u[:, cs] * acc).astype(BF16)

    _gates(hb, wg_ref, bg_ref, sg_ref)


def _proj_sample(h2d, t_len, ws4, sb4, wqkv, bqkv, wuv, buv, wg, bg, sln_g, sln_b):
    n = h2d.shape[0]
    assert t_len & (t_len - 1) == 0
    smem = pl.BlockSpec(memory_space=pltpu.SMEM)
    ins = (h2d, wqkv, bqkv, wuv, buv, wg, bg, sln_g, sln_b)
    out_shape = [jax.ShapeDtypeStruct((n, N_GROUPS * GW), F32),
                 jax.ShapeDtypeStruct((n, N_GROUPS * 2 * GW), F32),
                 jax.ShapeDtypeStruct((n, SGU_W), BF16),
                 jax.ShapeDtypeStruct((n, 2 * D_MODEL), BF16),
                 jax.ShapeDtypeStruct((n, SGU_W), F32)]
    return pl.pallas_call(
        functools.partial(_proj_sample_kernel, t_len),
        grid=(1,),
        in_specs=[smem, smem] + [_whole(a.shape) for a in ins],
        out_specs=[_whole(s.shape) for s in out_shape],
        out_shape=out_shape,
        compiler_params=pltpu.CompilerParams(dimension_semantics=("arbitrary",), vmem_limit_bytes=VMEM_LIMIT),
        name="proj_sample",
    )(ws4, sb4, *ins)


def _bias_mask(bmap, buckets, value_of_bucket):
    acc = jnp.full(bmap.shape, NEG, F32)
    for bkt in buckets:
        acc = jnp.where(bmap == bkt, value_of_bucket(bkt), acc)
    return acc


def _prompt_bucket_map(window, dil):
    band = window // dil
    rel = np.arange(BLK)[:, None] - np.arange(2 * BLK)[None, :] + BLK
    valid = (rel >= 0) & (rel <= band)
    return np.where(valid, _t5_bucket(np.clip(rel, 0, band) * dil), -1).astype(np.int32)


def _attn_prompt_kernel(buckets, tab_ref, bmap_ref, qkv0_ref, qkv1_ref, qkv2_ref,
                        ol0_ref, ol1_ref, ol2_ref, bm_ref):
    @pl.when(pl.program_id(0) == 0)
    def _():
        for g in range(N_GROUPS):
            bmap = bmap_ref[g]
            for h in range(GROUP_HEADS):
                rs = slice(h * BLK, (h + 1) * BLK)
                bm = _bias_mask(bmap, buckets[g], lambda bkt: tab_ref[bkt, g * GROUP_HEADS + h])
                bm_ref[2 * g, rs, :] = bm
                bm_ref[2 * g + 1, rs, :BLK] = bm[:, BLK:]
                bm_ref[2 * g + 1, rs, BLK:] = jnp.full((BLK, BLK), NEG, F32)

    lane_head = lax.broadcasted_iota(jnp.int32, (BLK, GW), 1) >> 6
    lane_lo = lax.broadcasted_iota(jnp.int32, (BLK, BLK), 1) < HEAD_DIM

    def unit(g, qkv_ref, ol_ref, r, n):
        single = qkv_ref.shape[2] == BLK
        static_n = isinstance(n, int)
        q0 = n * BLK if static_n else pl.multiple_of(n * BLK, BLK)
        q = qkv_ref[0, r, pl.ds(q0, BLK), 0:GW]
        qs = jnp.concatenate([jnp.where(lane_head == h, q, jnp.zeros_like(q))
                              for h in range(GROUP_HEADS)], axis=0)
        if single:
            k = qkv_ref[0, r, :, GW:2 * GW]
            v = qkv_ref[0, r, :, 2 * GW:3 * GW]
            bm = bm_ref[2 * g, :, BLK:]
        else:
            if static_n:
                k0 = max(q0 - BLK, 0)
                bm = bm_ref[2 * g + int(n == 0)]
            else:
                k0 = pl.multiple_of(jnp.maximum(q0 - BLK, 0), BLK)
                bm = bm_ref[2 * g + (n == 0).astype(jnp.int32)]
            k = qkv_ref[0, r, pl.ds(k0, 2 * BLK), GW:2 * GW]
            v = qkv_ref[0, r, pl.ds(k0, 2 * BLK), 2 * GW:3 * GW]
        s = lax.dot_general(qs, k, (((1,), (1,)), ((), ())), preferred_element_type=F32) + bm
        m = jnp.max(s, axis=-1, keepdims=True)
        p = jnp.exp(s - m)
        l = jnp.sum(p, axis=-1, keepdims=True)
        pb = p.astype(BF16)
        linv = 1.0 / l
        lse = m + jnp.log(l)
        rows = pl.ds(q0, BLK)
        for half in range(2):
            pv = jnp.dot(pb[2 * half * BLK:(2 * half + 2) * BLK], v[:, half * BLK:(half + 1) * BLK],
                         preferred_element_type=F32)
            ra = slice(2 * half * BLK, (2 * half + 1) * BLK)
            rb = slice((2 * half + 1) * BLK, (2 * half + 2) * BLK)
            o = jnp.where(lane_lo, pv[:BLK] * linv[ra], pv[BLK:] * linv[rb])
            ls = jnp.where(lane_lo, lse[ra], lse[rb])
            ol_ref[0, r, rows, half * BLK:(half + 1) * BLK] = o
            ol_ref[0, r, rows, GW + half * BLK:GW + (half + 1) * BLK] = ls

    unroll = 4
    for g, (qkv_ref, ol_ref) in enumerate(((qkv0_ref, ol0_ref), (qkv1_ref, ol1_ref), (qkv2_ref, ol2_ref))):
        dil, sub_len = qkv_ref.shape[1], qkv_ref.shape[2]
        nb = sub_len // BLK
        n_units = dil * nb
        assert n_units % unroll == 0 and (nb == 1 or nb == unroll or dil == 1)

        def trip(t, carry, g=g, qkv_ref=qkv_ref, ol_ref=ol_ref, nb=nb, dil=dil):
            for j in range(unroll):
                if nb == 1:
                    r, n = t * unroll + j, 0
                elif dil == 1:
                    r, n = 0, t * unroll + j
                else:
                    r, n = t, j
                unit(g, qkv_ref, ol_ref, r, n)
            return carry

        lax.fori_loop(0, n_units // unroll, trip, 0)


def _attn_prompt(tab, qkvs, batch):
    bmaps = [_prompt_bucket_map(w, d) for w, d in ATT_GROUPS]
    buckets = tuple(tuple(int(b) for b in np.unique(m) if b >= 0) for m in bmaps)
    bmap = jnp.asarray(np.stack(bmaps))
    in_specs = [pl.BlockSpec(memory_space=pltpu.SMEM), _whole(bmap.shape)]
    out_shape, out_specs = [], []
    for a in qkvs:
        _, dil, sub_len, _ = a.shape
        in_specs.append(pl.BlockSpec((1, dil, sub_len, 3 * GW), lambda b: (b, 0, 0, 0)))
        out_shape.append(jax.ShapeDtypeStruct((batch, dil, sub_len, 2 * GW), F32))
        out_specs.append(pl.BlockSpec((1, dil, sub_len, 2 * GW), lambda b: (b, 0, 0, 0)))
    return pl.pallas_call(
        functools.partial(_attn_prompt_kernel, buckets),
        grid=(batch,),
        in_specs=in_specs,
        out_specs=out_specs,
        out_shape=out_shape,
        scratch_shapes=[pltpu.VMEM((2 * N_GROUPS, GROUP_HEADS * BLK, 2 * BLK), F32)],
        compiler_params=pltpu.CompilerParams(dimension_semantics=("arbitrary",), vmem_limit_bytes=VMEM_LIMIT),
        name="attn_prompt",
    )(tab, bmap, *qkvs)


def _attn_sample_init(g, t_len, buckets, in_refs, out_refs, scratch_refs):
    tab_ref, bmap_ref = in_refs[:2]
    bm_ref, tr_ref, _ = scratch_refs
    rh = lax.broadcasted_iota(jnp.int32, bm_ref.shape, 0) // t_len

    def value(bkt):
        v = jnp.full(bm_ref.shape, tab_ref[bkt, g * GROUP_HEADS], F32)
        for h in range(1, GROUP_HEADS):
            v = jnp.where(rh == h, tab_ref[bkt, g * GROUP_HEADS + h], v)
        return v

    bm_ref[...] = _bias_mask(bmap_ref[...], buckets, value)
    tr_ref[...] = jnp.zeros_like(tr_ref)


def _attn_sample_step(t_len, in_refs, out_refs, scratch_refs):
    _, _, q_ref, kvnew_ref, st_ref = in_refs
    win_ref, ol_ref = out_refs
    bm_ref, tr_ref, fb_ref = scratch_refs
    lg = st_ref.shape[2]
    nrow = GROUP_HEADS * t_len
    row_head = lax.broadcasted_iota(jnp.int32, (nrow, GW), 0) // t_len
    lane_head = lax.broadcasted_iota(jnp.int32, (nrow, GW), 1) >> 6
    ext = lg + BLK
    sel = row_head == lane_head
    for i in range(st_ref.shape[0]):
        tr_ref[i, 0:t_len, :] = kvnew_ref[i]
        for c in range(2 * GW // BLK):
            rows = slice(c * BLK, (c + 1) * BLK)
            full = jnp.concatenate([st_ref[i, rows, :], tr_ref[i, :, rows].T], axis=1)
            win_ref[i, rows, :] = pltpu.roll(full, ext - t_len, axis=1)[:, :lg]
            fb_ref[i, rows, :] = full.astype(BF16)

        qs = jnp.where(sel, q_ref[i], 0.0).astype(BF16)
        s = jnp.dot(qs, fb_ref[i, 0:GW, :], preferred_element_type=F32) + bm_ref[...]
        m = jnp.max(s, axis=-1, keepdims=True)
        p = jnp.exp(s - m)
        l = jnp.sum(p, axis=-1, keepdims=True)
        pv = lax.dot_general(p.astype(BF16), fb_ref[i, GW:2 * GW, :], (((1,), (1,)), ((), ())),
                             preferred_element_type=F32) * (1.0 / l)
        lse = jnp.broadcast_to(m + jnp.log(l), (nrow, GW))
        o = jnp.where(sel, pv, 0.0)
        ls = jnp.where(sel, lse, 0.0)
        osum, lsum = o, ls
        for h in range(1, GROUP_HEADS):
            osum = osum + pltpu.roll(o, h * t_len, axis=0)
            lsum = lsum + pltpu.roll(ls, h * t_len, axis=0)
        ol_ref[i, :, 0:GW] = osum[0:t_len]
        ol_ref[i, :, GW:2 * GW] = lsum[0:t_len]


def _sample_bucket_map(window, dil, lg, t_len):
    band = window // dil
    ext = lg + BLK
    bmap = np.full((t_len, ext), -1, np.int64)
    t = np.arange(t_len)[:, None]
    j = np.arange(lg + t_len)[None, :]
    dist = lg + t - j
    valid = (dist >= 0) & (dist % dil == 0) & (dist <= band * dil)
    bmap[:, :lg + t_len] = np.where(valid, _t5_bucket(np.maximum(dist, 0)), -1)
    return np.tile(bmap, (GROUP_HEADS, 1)).astype(np.int32)


def _attn_sample_side(g, tab, q16, kvnew, st, n_steps, step_of):
    window, dil = ATT_GROUPS[g]
    nb, _, lg = st.shape
    assert lg == window and nb % n_steps == 0
    per = nb // n_steps
    t_len = kvnew.shape[1]
    nrow = GROUP_HEADS * t_len
    bmap_np = _sample_bucket_map(window, dil, lg, t_len)
    buckets = tuple(int(b) for b in np.unique(bmap_np) if b >= 0)
    bmap = jnp.asarray(bmap_np)

    def blk(*idx):
        return (step_of(*idx), 0, 0)

    return _Side(
        name="attn_sample_g%d" % g,
        init=functools.partial(_attn_sample_init, g, t_len, buckets),
        step=functools.partial(_attn_sample_step, t_len),
        inputs=(tab, bmap, q16, kvnew, st),
        in_specs=(pl.BlockSpec(memory_space=pltpu.SMEM), _whole(bmap.shape),
                  pl.BlockSpec((per, nrow, GW), blk),
                  pl.BlockSpec((per, t_len, 2 * GW), blk),
                  pl.BlockSpec((per, 2 * GW, lg), blk)),
        out_shape=(jax.ShapeDtypeStruct((nb, 2 * GW, lg), F32),
                   jax.ShapeDtypeStruct((nb, t_len, 2 * GW), F32)),
        out_specs=(pl.BlockSpec((per, 2 * GW, lg), blk),
                   pl.BlockSpec((per, t_len, 2 * GW), blk)),
        scratch_shapes=(pltpu.VMEM((nrow, lg + BLK), F32), pltpu.VMEM((per, BLK, 2 * GW), F32),
                        pltpu.VMEM((per, 2 * GW, lg + BLK), BF16)))


def _mix_main(alpha, in_refs, out_refs, scratch_refs):
    (h_ref, ol0_ref, ol1_ref, ol2_ref, yb_ref, sg_ref, woa_ref, wob_ref, wout_ref, g_ref, b_ref) = in_refs
    (o_ref,) = out_refs
    (u_ref,) = scratch_refs
    ol_refs = (ol0_ref, ol1_ref, ol2_ref)
    tm = h_ref.shape[0]
    ncol = 2 * GW // BLK

    for gi, ol_ref in enumerate(ol_refs):
        dil = ol_ref.shape[1]
        if dil > 1:
            for r in range(dil):
                for j in range(ncol):
                    u_ref[gi, j, pl.ds(r, tm // dil, stride=dil), :] = ol_ref[0, r, :, j * BLK:(j + 1) * BLK]

    def natural(gi, rows, lo, hi):
        if ol_refs[gi].shape[1] == 1:
            return ol_refs[gi][0, 0, rows, lo * BLK:hi * BLK]
        return jnp.concatenate([u_ref[gi, j, rows, :] for j in range(lo, hi)], axis=1)

    chunks = [slice(c * (tm // MIX_CHUNKS), (c + 1) * (tm // MIX_CHUNKS)) for c in range(MIX_CHUNKS)]

    def attend(rows):
        outs = [natural(gi, rows, 0, ncol // 2) for gi in range(N_GROUPS)]
        lses = [natural(gi, rows, ncol // 2, ncol) for gi in range(N_GROUPS)]
        mx = jnp.maximum(jnp.maximum(lses[0], lses[1]), lses[2])
        es = [jnp.exp(ls - mx) for ls in lses]
        den = es[0] + es[1] + es[2]
        num = es[0] * outs[0] + es[1] * outs[1] + es[2] * outs[2]
        return (num * (1.0 / den)).astype(BF16)

    atts = [attend(rows) for rows in chunks]
    gateds = []
    for rows, att in zip(chunks, atts):
        a = jnp.dot(att, woa_ref[...], preferred_element_type=F32)
        bb = jnp.dot(yb_ref[rows, :], wob_ref[...], preferred_element_type=F32)
        gated = sg_ref[rows, :D_MODEL].astype(F32) * a + sg_ref[rows, D_MODEL:].astype(F32) * bb
        gateds.append(gated.astype(BF16))
    for rows, gated in zip(chunks, gateds):
        mix = jnp.dot(gated, wout_ref[...], preferred_element_type=F32)
        o_ref[rows, :] = _ln(alpha * h_ref[rows, :] + mix, g_ref[...], b_ref[...])


def _mix(h2d, ols, yb, sg, woa, wob, wout, g, b, alpha, batch, seq):
    tm = min(TM, seq)
    nt = seq // tm
    ins = [h2d, *ols, yb, sg, woa, wob, wout, g, b]
    in_specs = [pl.BlockSpec((tm, D_MODEL), lambda bi, t: (bi * nt + t, 0))]
    for a in ols:
        dil = a.shape[1]
        in_specs.append(pl.BlockSpec((1, dil, tm // dil, 2 * GW), lambda bi, t: (bi, 0, t, 0)))
    in_specs += [pl.BlockSpec((tm, SGU_W), lambda bi, t: (bi * nt + t, 0)),
                 pl.BlockSpec((tm, 2 * D_MODEL), lambda bi, t: (bi * nt + t, 0))]
    in_specs += [_whole(a.shape) for a in (woa, wob, wout, g, b)]
    (out,), _ = _call(
        "mix", functools.partial(_mix_main, alpha), None, (batch, nt), ins, in_specs,
        [jax.ShapeDtypeStruct((batch * seq, D_MODEL), F32)],
        [pl.BlockSpec((tm, D_MODEL), lambda bi, t: (bi * nt + t, 0))],
        [pltpu.VMEM((N_GROUPS, 2 * GW // BLK, tm, BLK), F32)])
    return out


def _prep_ffn(w1, w3, w2):
    w13 = jnp.concatenate([w1.reshape(D_MODEL, NC, FC), w3.reshape(D_MODEL, NC, FC)], axis=-1)
    return (jnp.transpose(w13, (1, 0, 2)).astype(BF16), w2.reshape(NC, FC, D_MODEL).astype(BF16))


def _prep_w_in(w_in, b_in):
    def cols(a, lo, n):
        return a[..., lo:lo + n]
    wq, bq = [], []
    for g in range(N_GROUPS):
        parts = [(i * ATT_W + g * GW, GW) for i in range(3)]
        wq.append(jnp.concatenate([cols(w_in, lo, n) for lo, n in parts], axis=-1))
        bq.append(jnp.concatenate([cols(b_in, lo, n) for lo, n in parts], axis=-1))
    o_uv = 3 * ATT_W
    return (jnp.stack(wq).astype(BF16), jnp.stack(bq)[:, None, :],
            cols(w_in, o_uv, 2 * SGU_W).astype(BF16), cols(b_in, o_uv, 2 * SGU_W)[None, :],
            cols(w_in, o_uv + 2 * SGU_W, 2 * D_MODEL).astype(BF16),
            cols(b_in, o_uv + 2 * SGU_W, 2 * D_MODEL)[None, :])


def _to_window(a):
    return a.reshape(a.shape[0], a.shape[1], 2, GROUP_HEADS, HEAD_DIM)


def kernel(x_prompt, x_sample, state_win0, state_win1, state_win2, rel_bias, ln1_g, ln1_b, f1_w1, f1_w3, f1_w2, w_in, b_in, sgu_ln_g, sgu_ln_b, sgu_ws, sgu_b, w_oa, w_ob, w_out, ln2_g, ln2_b, f2_w1, f2_w3, f2_w2, ln3_g, ln3_b):
    depth = ln1_g.shape[0]
    alpha = (2 * depth) ** 0.25
    batch, seq, _ = x_prompt.shape
    dec_batch, dec_seq, _ = x_sample.shape
    states = (state_win0, state_win1, state_win2)

    yp = x_prompt.reshape(batch * seq, D_MODEL)
    ys = x_sample.reshape(dec_batch * dec_seq, D_MODEL)
    outs = [[] for _ in range(7)]
    for l in range(depth):
        f1 = _prep_ffn(f1_w1[l], f1_w3[l], f1_w2[l])
        f2 = _prep_ffn(f2_w1[l], f2_w3[l], f2_w2[l])
        wqkv, bqkv, wuv, buv, wg, bg = _prep_w_in(w_in[l], b_in[l])
        sln = (sgu_ln_g[l][None, :], sgu_ln_b[l][None, :])
        woa, wob, wout = w_oa[l].astype(BF16), w_ob[l].astype(BF16), w_out[l].astype(BF16)
        ln1 = (ln1_g[l][None, :], ln1_b[l][None, :])
        ln2 = (ln2_g[l][None, :], ln2_b[l][None, :])
        ln3 = (ln3_g[l][None, :], ln3_b[l][None, :])

        hs, _ = _ffn_ln(ys, *f1, *ln1, alpha)
        ws4 = sgu_ws[l][:, :dec_seq, :dec_seq].reshape(SGU_GROUPS, dec_seq * dec_seq)
        sb4 = sgu_b[l][:, :dec_seq]
        qs, kvs, ybs, sgs, vn = _proj_sample(hs, dec_seq, ws4, sb4, wqkv, bqkv, wuv, buv, wg, bg, *sln)
        nt = seq // TM
        n_steps = batch * nt
        sides = []
        for g in range(N_GROUPS):
            st = states[g][l]
            lg = st.shape[1]
            st_cm = jnp.transpose(st, (0, 2, 3, 4, 1)).reshape(dec_batch, 2 * GW, lg)
            q16 = jnp.tile(qs[:, g * GW:(g + 1) * GW].reshape(dec_batch, 1, dec_seq, GW),
                           (1, GROUP_HEADS, 1, 1)).reshape(dec_batch, GROUP_HEADS * dec_seq, GW)
            kvnew = kvs[:, g * 2 * GW:(g + 1) * 2 * GW].reshape(dec_batch, dec_seq, 2 * GW)
            step_of = (lambda i: i) if g == N_GROUPS - 1 else (lambda b, t: b * nt + t)
            sides.append(_attn_sample_side(g, rel_bias, q16, kvnew, st_cm, n_steps, step_of))

        hp, side_a = _ffn_ln(yp, *f1, *ln1, alpha, sides=sides[2:])
        res, side_b = _proj_prompt(hp, batch, seq, wqkv, bqkv, wuv, buv, wg, bg, *sln,
                                   sgu_ws[l], jnp.transpose(sgu_b[l]), sides=sides[:2])
        qkvs, wins, ybp, sgp = res[0:3], res[3:6], res[6], res[7]
        ols = _attn_prompt(rel_bias, qkvs, batch)
        h2 = _mix(hp, ols, ybp, sgp, woa, wob, wout, *ln2, alpha, batch, seq)
        yp, _ = _ffn_ln(h2, *f2, *ln3, alpha)
        for g in range(N_GROUPS):
            outs[g].append(_to_window(wins[g]))

        ols_s = []
        for g, (win_cm, ol) in enumerate(side_b + side_a):
            lg = win_cm.shape[2]
            win = jnp.transpose(win_cm.reshape(dec_batch, 2, GROUP_HEADS, HEAD_DIM, lg), (0, 4, 1, 2, 3))
            outs[3 + g].append(win)
            ols_s.append(ol.reshape(1, 1, dec_batch * dec_seq, 2 * GW))
        h2s = _mix(hs, ols_s, ybs, sgs, woa, wob, wout, *ln2, alpha, 1, dec_batch * dec_seq)
        ys, _ = _ffn_ln(h2s, *f2, *ln3, alpha)
        outs[6].append(vn.reshape(dec_batch, dec_seq, SGU_W))

    return (yp.reshape(batch, seq, D_MODEL), ys.reshape(dec_batch, dec_seq, D_MODEL),
            *[jnp.stack(o) for o in outs])
```

```python
import functools
import math
from typing import Callable, NamedTuple

import jax
import jax.numpy as jnp
import numpy as np
from jax import lax
from jax.experimental import pallas as pl
from jax.experimental.pallas import tpu as pltpu

F32 = jnp.float32
BF16 = jnp.bfloat16

D_MODEL = 1024
HEAD_DIM = 64
GROUP_HEADS = 4
ATT_GROUPS = ((128, 1), (512, 4), (2048, 16))
N_GROUPS = len(ATT_GROUPS)
N_ATT_HEADS = N_GROUPS * GROUP_HEADS
ATT_W = N_ATT_HEADS * HEAD_DIM
GW = GROUP_HEADS * HEAD_DIM
BLK = 128
N_BUCKETS = 32
MAX_DISTANCE = 2048
SGU_CHUNK = 128
SGU_GROUPS = 4
SGU_GROUP_W = 128
SGU_W = SGU_GROUPS * SGU_GROUP_W
D_FF = 2816
LN_EPS = 1e-5
NEG = -1e30
SCALE = HEAD_DIM ** -0.5

TM = 512
FC = 256
NC = D_FF // FC
VMEM_LIMIT = 60 * 1024 * 1024


def _t5_bucket(dist):
    dist = np.asarray(dist, np.int64)
    max_exact = N_BUCKETS // 2
    large = max_exact + (np.log(np.maximum(dist, max_exact) / max_exact)
                         / np.log(MAX_DISTANCE / max_exact) * (N_BUCKETS - max_exact)).astype(np.int64)
    large = np.minimum(large, N_BUCKETS - 1)
    return np.where(dist < max_exact, dist, large).astype(np.int32)


def _ln(t, g, b):
    mu = jnp.mean(t, axis=-1, keepdims=True)
    d = t - mu
    var = jnp.mean(d * d, axis=-1, keepdims=True)
    return d * lax.rsqrt(var + LN_EPS) * g + b


def _whole(shape):
    nd = len(shape)
    return pl.BlockSpec(shape, lambda *_: (0,) * nd, pipeline_mode=pl.Buffered(1))


class _Side(NamedTuple):
    name: str
    init: Callable
    step: Callable
    inputs: tuple
    in_specs: tuple
    out_shape: tuple
    out_specs: tuple
    scratch_shapes: tuple


def _take(refs, pos, count):
    return refs[pos:pos + count], pos + count


def _call(name, main, main_init, grid, inputs, in_specs, out_shape, out_specs, scratch, sides=()):
    n_in, n_out, n_scr = len(inputs), len(out_shape), len(scratch)

    def body(*refs):
        refs = list(refs)
        m_in, pos = _take(refs, 0, n_in)
        s_in = []
        for s in sides:
            r, pos = _take(refs, pos, len(s.inputs))
            s_in.append(r)
        m_out, pos = _take(refs, pos, n_out)
        s_out = []
        for s in sides:
            r, pos = _take(refs, pos, len(s.out_shape))
            s_out.append(r)
        m_scr, pos = _take(refs, pos, n_scr)
        s_scr = []
        for s in sides:
            r, pos = _take(refs, pos, len(s.scratch_shapes))
            s_scr.append(r)

        if main_init is not None or sides:
            first = pl.program_id(0) == 0
            for axis in range(1, len(grid)):
                first = jnp.logical_and(first, pl.program_id(axis) == 0)

            @pl.when(first)
            def _():
                if main_init is not None:
                    main_init(m_in, m_out, m_scr)
                for s, i, o, c in zip(sides, s_in, s_out, s_scr):
                    s.init(i, o, c)

        main(m_in, m_out, m_scr)
        for s, i, o, c in zip(sides, s_in, s_out, s_scr):
            s.step(i, o, c)

    all_in = list(inputs) + [a for s in sides for a in s.inputs]
    all_in_specs = list(in_specs) + [a for s in sides for a in s.in_specs]
    all_out_shape = list(out_shape) + [a for s in sides for a in s.out_shape]
    all_out_specs = list(out_specs) + [a for s in sides for a in s.out_specs]
    all_scratch = list(scratch) + [a for s in sides for a in s.scratch_shapes]
    res = pl.pallas_call(
        body,
        grid=grid,
        in_specs=all_in_specs,
        out_specs=all_out_specs,
        out_shape=all_out_shape,
        scratch_shapes=all_scratch,
        compiler_params=pltpu.CompilerParams(dimension_semantics=("arbitrary",) * len(grid),
                                             vmem_limit_bytes=VMEM_LIMIT),
        name=name + "".join("_" + s.name for s in sides),
    )(*all_in)
    res = list(res)
    main_res, pos = _take(res, 0, n_out)
    side_res = []
    for s in sides:
        r, pos = _take(res, pos, len(s.out_shape))
        side_res.append(r)
    return main_res, side_res


def _swiglu_ln(alpha, x_ref, xb_ref, w13_ref, w2_ref, g_ref, b_ref, acc_ref, o_ref):
    for c in range(NC):
        cols = slice(c * FC, (c + 1) * FC)
        a = jnp.dot(xb_ref[...], w13_ref[0, :, cols], preferred_element_type=F32)
        b = jnp.dot(xb_ref[...], w13_ref[1, :, cols], preferred_element_type=F32)
        hid = (a * jax.nn.sigmoid(a)) * b
        part = jnp.dot(hid.astype(BF16), w2_ref[c], preferred_element_type=F32)
        if c == 0:
            acc_ref[...] = part
        else:
            acc_ref[...] += part
    t = alpha * x_ref[...] + 0.5 * acc_ref[...]
    o_ref[...] = _ln(t, g_ref[...], b_ref[...])


def _ffn_specs():
    return [_whole((2, D_MODEL, D_FF)), _whole((NC, FC, D_MODEL)), _whole((1, D_MODEL)), _whole((1, D_MODEL))]


def _ffn_ln_main(alpha, in_refs, out_refs, scratch_refs):
    x_ref, w13_ref, w2_ref, g_ref, b_ref = in_refs
    xb_ref, acc_ref = scratch_refs
    xb_ref[...] = x_ref[...].astype(BF16)
    _swiglu_ln(alpha, x_ref, xb_ref, w13_ref, w2_ref, g_ref, b_ref, acc_ref, out_refs[0])


def _ffn_ln(x2d, w13, w2, g, b, alpha, sides=()):
    n = x2d.shape[0]
    assert n % TM == 0
    (out,), side_res = _call(
        "ffn_ln", functools.partial(_ffn_ln_main, alpha), None, (n // TM,),
        [x2d, w13, w2, g, b],
        [pl.BlockSpec((TM, D_MODEL), lambda i: (i, 0))] + _ffn_specs(),
        [jax.ShapeDtypeStruct((n, D_MODEL), F32)],
        [pl.BlockSpec((TM, D_MODEL), lambda i: (i, 0))],
        [pltpu.VMEM((TM, D_MODEL), BF16), pltpu.VMEM((TM, D_MODEL), F32)],
        sides)
    return out, side_res


def _gelu_exact(z):
    return 0.5 * z * (1.0 + lax.erf(z * (1.0 / math.sqrt(2.0))))


def _uv_branch(hb, wuv_ref, buv_ref, sg_ref, sb_ref):
    zuv = jnp.dot(hb, wuv_ref[...], preferred_element_type=F32) + buv_ref[...]
    zz = _gelu_exact(zuv)
    u = zz[:, :SGU_W]
    vn = _ln(zz[:, SGU_W:], sg_ref[...], sb_ref[...])
    return u, vn


def _gates(hb, wg_ref, bg_ref, sg_out_ref):
    zg = jnp.dot(hb, wg_ref[...], preferred_element_type=F32) + bg_ref[...]
    sg_out_ref[...] = jax.nn.sigmoid(zg).astype(BF16)


def _proj_prompt_main(in_refs, out_refs, scratch_refs):
    (h_ref, wqkv_ref, bqkv_ref, wuv_ref, buv_ref, wg_ref, bg_ref,
     sln_g_ref, sln_b_ref, ws_ref, sbt_ref) = in_refs
    qkv_refs, win_refs = out_refs[0:3], out_refs[3:6]
    yb_ref, sg_ref = out_refs[6:8]
    (zs_ref,) = scratch_refs
    hb = h_ref[...].astype(BF16)
    u, vn = _uv_branch(hb, wuv_ref, buv_ref, sln_g_ref, sln_b_ref)
    _gates(hb, wg_ref, bg_ref, sg_ref)
    for g, (window, dil) in enumerate(ATT_GROUPS):
        zg = jnp.dot(hb, wqkv_ref[g], preferred_element_type=F32) + bqkv_ref[g]
        keep = min(window, TM)
        win_refs[g][0] = zg[TM - keep:, GW:]
        zq = jnp.concatenate([zg[:, :GW] * SCALE, zg[:, GW:]], axis=1)
        if dil == 1:
            qkv_refs[g][0, 0] = zq.astype(BF16)
            continue
        ncol = 3 * GW // BLK
        for j in range(ncol):
            zs_ref[j] = zq[:, j * BLK:(j + 1) * BLK]
        rows = TM // dil
        for r in range(dil):
            sub = jnp.concatenate([zs_ref[j, pl.ds(r, rows, stride=dil), :] for j in range(ncol)], axis=1)
            qkv_refs[g][0, r] = sub.astype(BF16)

    vnb = vn.astype(BF16)
    row = lax.broadcasted_iota(jnp.int32, (SGU_CHUNK, SGU_CHUNK), 0)
    col = lax.broadcasted_iota(jnp.int32, (SGU_CHUNK, SGU_CHUNK), 1)
    for g in range(SGU_GROUPS):
        wm = jnp.where(row >= col, ws_ref[g], 0.0).astype(BF16)
        bias = jnp.broadcast_to(sbt_ref[:, g:g + 1], (SGU_CHUNK, SGU_GROUP_W))
        cs = slice(g * SGU_GROUP_W, (g + 1) * SGU_GROUP_W)
        for c in range(TM // SGU_CHUNK):
            rs = slice(c * SGU_CHUNK, (c + 1) * SGU_CHUNK)
            mixed = jnp.dot(wm, vnb[rs, cs], preferred_element_type=F32) + bias
            yb_ref[rs, cs] = (u[rs, cs] * mixed).astype(BF16)


def _proj_prompt(h2d, batch, seq, wqkv, bqkv, wuv, buv, wg, bg, sln_g, sln_b, ws, sbt, sides=()):
    nt = seq // TM
    assert seq % TM == 0 and TM == ATT_GROUPS[1][0] and TM >= ATT_GROUPS[0][0]
    ins = [h2d, wqkv, bqkv, wuv, buv, wg, bg, sln_g, sln_b, ws, sbt]
    in_specs = [pl.BlockSpec((TM, D_MODEL), lambda b, t: (b * nt + t, 0))] + [_whole(a.shape) for a in ins[1:]]
    out_shape, out_specs = [], []
    for window, dil in ATT_GROUPS:
        out_shape.append(jax.ShapeDtypeStruct((batch, dil, seq // dil, 3 * GW), BF16))
        out_specs.append(pl.BlockSpec((1, dil, TM // dil, 3 * GW), lambda b, t: (b, 0, t, 0)))
    for window, dil in ATT_GROUPS:
        keep = min(window, seq)
        out_shape.append(jax.ShapeDtypeStruct((batch, keep, 2 * GW), F32))
        if keep > TM:
            assert keep == seq
            out_specs.append(pl.BlockSpec((1, TM, 2 * GW), lambda b, t: (b, t, 0)))
        else:
            out_specs.append(pl.BlockSpec((1, keep, 2 * GW), lambda b, t: (b, 0, 0)))
    out_shape += [jax.ShapeDtypeStruct((batch * seq, SGU_W), BF16),
                  jax.ShapeDtypeStruct((batch * seq, 2 * D_MODEL), BF16)]
    out_specs += [pl.BlockSpec((TM, SGU_W), lambda b, t: (b * nt + t, 0)),
                  pl.BlockSpec((TM, 2 * D_MODEL), lambda b, t: (b * nt + t, 0))]
    return _call("proj_prompt", _proj_prompt_main, None, (batch, nt), ins, in_specs, out_shape, out_specs,
                 [pltpu.VMEM((3 * GW // BLK, TM, BLK), F32)], sides)


def _proj_sample_kernel(t_len, ws4_ref, sb4_ref, h_ref, wqkv_ref, bqkv_ref, wuv_ref, buv_ref,
                        wg_ref, bg_ref, sln_g_ref, sln_b_ref,
                        q_ref, kv_ref, yb_ref, sg_ref, vn_ref):
    n = h_ref.shape[0]
    hb = h_ref[...].astype(BF16)
    for g in range(N_GROUPS):
        zg = jnp.dot(hb, wqkv_ref[g], preferred_element_type=F32) + bqkv_ref[g]
        q_ref[:, g * GW:(g + 1) * GW] = zg[:, :GW] * SCALE
        kv_ref[:, g * 2 * GW:(g + 1) * 2 * GW] = zg[:, GW:]

    u, vn = _uv_branch(hb, wuv_ref, buv_ref, sln_g_ref, sln_b_ref)
    vn_ref[...] = vn
    p = lax.broadcasted_iota(jnp.int32, (n, SGU_GROUP_W), 0) & (t_len - 1)
    for g in range(SGU_GROUPS):
        cs = slice(g * SGU_GROUP_W, (g + 1) * SGU_GROUP_W)
        vg = vn[:, cs]
        acc = jnp.zeros((n, SGU_GROUP_W), F32)
        for pp in range(t_len):
            acc = jnp.where(p == pp, sb4_ref[g, pp], acc)
        for d in range(t_len):
            coef = jnp.zeros((n, SGU_GROUP_W), F32)
            for pp in range(d, t_len):
                coef = jnp.where(p == pp, ws4_ref[g, pp * t_len + (pp - d)], coef)
            shifted = vg if d == 0 else pltpu.roll(vg, d, axis=0)
            acc = acc + coef * shifted
        yb_ref[:, cs] = (u[:, cs] * acc).astype(BF16)

    _gates(hb, wg_ref, bg_ref, sg_ref)


def _proj_sample(h2d, t_len, ws4, sb4, wqkv, bqkv, wuv, buv, wg, bg, sln_g, sln_b):
    n = h2d.shape[0]
    assert t_len & (t_len - 1) == 0
    smem = pl.BlockSpec(memory_space=pltpu.SMEM)
    ins = (h2d, wqkv, bqkv, wuv, buv, wg, bg, sln_g, sln_b)
    out_shape = [jax.ShapeDtypeStruct((n, N_GROUPS * GW), F32),
                 jax.ShapeDtypeStruct((n, N_GROUPS * 2 * GW), F32),
                 jax.ShapeDtypeStruct((n, SGU_W), BF16),
                 jax.ShapeDtypeStruct((n, 2 * D_MODEL), BF16),
                 jax.ShapeDtypeStruct((n, SGU_W), F32)]
    return pl.pallas_call(
        functools.partial(_proj_sample_kernel, t_len),
        grid=(1,),
        in_specs=[smem, smem] + [_whole(a.shape) for a in ins],
        out_specs=[_whole(s.shape) for s in out_shape],
        out_shape=out_shape,
        compiler_params=pltpu.CompilerParams(dimension_semantics=("arbitrary",), vmem_limit_bytes=VMEM_LIMIT),
        name="proj_sample",
    )(ws4, sb4, *ins)


def _bias_mask(bmap, buckets, value_of_bucket):
    acc = jnp.full(bmap.shape, NEG, F32)
    for bkt in buckets:
        acc = jnp.where(bmap == bkt, value_of_bucket(bkt), acc)
    return acc


def _prompt_bucket_map(window, dil):
    band = window // dil
    rel = np.arange(BLK)[:, None] - np.arange(2 * BLK)[None, :] + BLK
    valid = (rel >= 0) & (rel <= band)
    return np.where(valid, _t5_bucket(np.clip(rel, 0, band) * dil), -1).astype(np.int32)


def _combine_groups(outs, lses):
    mx = jnp.maximum(jnp.maximum(lses[0], lses[1]), lses[2])
    es = [jnp.exp(ls - mx) for ls in lses]
    den = es[0] + es[1] + es[2]
    num = es[0] * outs[0] + es[1] * outs[1] + es[2] * outs[2]
    return num * (1.0 / den)


def _attn_prompt_kernel(buckets, tab_ref, bmap_ref, qkv0_ref, qkv1_ref, qkv2_ref, att_ref, bm_ref, nat_ref):
    @pl.when(pl.program_id(0) == 0)
    def _():
        for g in range(N_GROUPS):
            bmap = bmap_ref[g]
            for h in range(GROUP_HEADS):
                rs = slice(h * BLK, (h + 1) * BLK)
                bm = _bias_mask(bmap, buckets[g], lambda bkt: tab_ref[bkt, g * GROUP_HEADS + h])
                bm_ref[2 * g, rs, :] = bm
                bm_ref[2 * g + 1, rs, :BLK] = bm[:, BLK:]
                bm_ref[2 * g + 1, rs, BLK:] = jnp.full((BLK, BLK), NEG, F32)

    lane_head = lax.broadcasted_iota(jnp.int32, (BLK, GW), 1) >> 6
    lane_lo = lax.broadcasted_iota(jnp.int32, (BLK, BLK), 1) < HEAD_DIM

    def unit(g, qkv_ref, r, n):
        dil = qkv_ref.shape[1]
        single = qkv_ref.shape[2] == BLK
        static_n = isinstance(n, int)
        q0 = n * BLK if static_n else pl.multiple_of(n * BLK, BLK)
        q = qkv_ref[0, r, pl.ds(q0, BLK), 0:GW]
        qs = jnp.concatenate([jnp.where(lane_head == h, q, jnp.zeros_like(q))
                              for h in range(GROUP_HEADS)], axis=0)
        if single:
            k = qkv_ref[0, r, :, GW:2 * GW]
            v = qkv_ref[0, r, :, 2 * GW:3 * GW]
            bm = bm_ref[2 * g, :, BLK:]
        else:
            if static_n:
                k0 = max(q0 - BLK, 0)
                bm = bm_ref[2 * g + int(n == 0)]
            else:
                k0 = pl.multiple_of(jnp.maximum(q0 - BLK, 0), BLK)
                bm = bm_ref[2 * g + (n == 0).astype(jnp.int32)]
            k = qkv_ref[0, r, pl.ds(k0, 2 * BLK), GW:2 * GW]
            v = qkv_ref[0, r, pl.ds(k0, 2 * BLK), 2 * GW:3 * GW]
        s = lax.dot_general(qs, k, (((1,), (1,)), ((), ())), preferred_element_type=F32) + bm
        m = jnp.max(s, axis=-1, keepdims=True)
        p = jnp.exp(s - m)
        l = jnp.sum(p, axis=-1, keepdims=True)
        pb = p.astype(BF16)
        linv = 1.0 / l
        lse = m + jnp.log(l)
        if dil == 1:
            rows = pl.ds(q0, BLK)
        else:
            rows = pl.ds(q0 * dil + r, BLK, stride=dil)
        for half in range(2):
            pv = jnp.dot(pb[2 * half * BLK:(2 * half + 2) * BLK], v[:, half * BLK:(half + 1) * BLK],
                         preferred_element_type=F32)
            ra = slice(2 * half * BLK, (2 * half + 1) * BLK)
            rb = slice((2 * half + 1) * BLK, (2 * half + 2) * BLK)
            nat_ref[g, half, rows, :] = jnp.where(lane_lo, pv[:BLK] * linv[ra], pv[BLK:] * linv[rb])
            nat_ref[g, 2 + half, rows, :] = jnp.where(lane_lo, lse[ra], lse[rb])

    unroll = 4
    for g, qkv_ref in enumerate((qkv0_ref, qkv1_ref, qkv2_ref)):
        dil, sub_len = qkv_ref.shape[1], qkv_ref.shape[2]
        nb = sub_len // BLK
        n_units = dil * nb
        assert n_units % unroll == 0 and (nb == 1 or nb == unroll or dil == 1)

        def trip(t, carry, g=g, qkv_ref=qkv_ref, nb=nb, dil=dil):
            for j in range(unroll):
                if nb == 1:
                    r, n = t * unroll + j, 0
                elif dil == 1:
                    r, n = 0, t * unroll + j
                else:
                    r, n = t, j
                unit(g, qkv_ref, r, n)
            return carry

        lax.fori_loop(0, n_units // unroll, trip, 0)

    seq = att_ref.shape[1]
    rows_per = 2 * BLK

    def combine(c, carry):
        rows = pl.ds(pl.multiple_of(c * rows_per, rows_per), rows_per)
        outs = [jnp.concatenate([nat_ref[g, 0, rows, :], nat_ref[g, 1, rows, :]], axis=1) for g in range(N_GROUPS)]
        lses = [jnp.concatenate([nat_ref[g, 2, rows, :], nat_ref[g, 3, rows, :]], axis=1) for g in range(N_GROUPS)]
        att_ref[0, rows, :] = _combine_groups(outs, lses).astype(BF16)
        return carry

    lax.fori_loop(0, seq // rows_per, combine, 0)


def _attn_prompt(tab, qkvs, batch):
    bmaps = [_prompt_bucket_map(w, d) for w, d in ATT_GROUPS]
    buckets = tuple(tuple(int(b) for b in np.unique(m) if b >= 0) for m in bmaps)
    bmap = jnp.asarray(np.stack(bmaps))
    in_specs = [pl.BlockSpec(memory_space=pltpu.SMEM), _whole(bmap.shape)]
    for a in qkvs:
        _, dil, sub_len, _ = a.shape
        in_specs.append(pl.BlockSpec((1, dil, sub_len, 3 * GW), lambda b: (b, 0, 0, 0)))
    seq = qkvs[0].shape[1] * qkvs[0].shape[2]
    return pl.pallas_call(
        functools.partial(_attn_prompt_kernel, buckets),
        grid=(batch,),
        in_specs=in_specs,
        out_specs=pl.BlockSpec((1, seq, GW), lambda b: (b, 0, 0)),
        out_shape=jax.ShapeDtypeStruct((batch, seq, GW), BF16),
        scratch_shapes=[pltpu.VMEM((2 * N_GROUPS, GROUP_HEADS * BLK, 2 * BLK), F32),
                        pltpu.VMEM((N_GROUPS, 2 * GW // BLK, seq, BLK), F32)],
        compiler_params=pltpu.CompilerParams(dimension_semantics=("arbitrary",), vmem_limit_bytes=VMEM_LIMIT),
        name="attn_prompt",
    )(tab, bmap, *qkvs)


def _attn_sample_init(g, t_len, buckets, in_refs, out_refs, scratch_refs):
    tab_ref, bmap_ref = in_refs[:2]
    bm_ref, tr_ref, _ = scratch_refs
    rh = lax.broadcasted_iota(jnp.int32, bm_ref.shape, 0) // t_len

    def value(bkt):
        v = jnp.full(bm_ref.shape, tab_ref[bkt, g * GROUP_HEADS], F32)
        for h in range(1, GROUP_HEADS):
            v = jnp.where(rh == h, tab_ref[bkt, g * GROUP_HEADS + h], v)
        return v

    bm_ref[...] = _bias_mask(bmap_ref[...], buckets, value)
    tr_ref[...] = jnp.zeros_like(tr_ref)


def _attn_sample_step(t_len, in_refs, out_refs, scratch_refs):
    _, _, q_ref, kvnew_ref, st_ref = in_refs
    win_ref, ol_ref = out_refs
    bm_ref, tr_ref, fb_ref = scratch_refs
    lg = st_ref.shape[2]
    nrow = GROUP_HEADS * t_len
    row_head = lax.broadcasted_iota(jnp.int32, (nrow, GW), 0) // t_len
    lane_head = lax.broadcasted_iota(jnp.int32, (nrow, GW), 1) >> 6
    ext = lg + BLK
    sel = row_head == lane_head
    for i in range(st_ref.shape[0]):
        tr_ref[i, 0:t_len, :] = kvnew_ref[i]
        for c in range(2 * GW // BLK):
            rows = slice(c * BLK, (c + 1) * BLK)
            full = jnp.concatenate([st_ref[i, rows, :], tr_ref[i, :, rows].T], axis=1)
            win_ref[i, rows, :] = pltpu.roll(full, ext - t_len, axis=1)[:, :lg]
            fb_ref[i, rows, :] = full.astype(BF16)

        qs = jnp.where(sel, q_ref[i], 0.0).astype(BF16)
        s = jnp.dot(qs, fb_ref[i, 0:GW, :], preferred_element_type=F32) + bm_ref[...]
        m = jnp.max(s, axis=-1, keepdims=True)
        p = jnp.exp(s - m)
        l = jnp.sum(p, axis=-1, keepdims=True)
        pv = lax.dot_general(p.astype(BF16), fb_ref[i, GW:2 * GW, :], (((1,), (1,)), ((), ())),
                             preferred_element_type=F32) * (1.0 / l)
        lse = jnp.broadcast_to(m + jnp.log(l), (nrow, GW))
        o = jnp.where(sel, pv, 0.0)
        ls = jnp.where(sel, lse, 0.0)
        osum, lsum = o, ls
        for h in range(1, GROUP_HEADS):
            osum = osum + pltpu.roll(o, h * t_len, axis=0)
            lsum = lsum + pltpu.roll(ls, h * t_len, axis=0)
        ol_ref[i, :, 0:GW] = osum[0:t_len]
        ol_ref[i, :, GW:2 * GW] = lsum[0:t_len]


def _sample_bucket_map(window, dil, lg, t_len):
    band = window // dil
    ext = lg + BLK
    bmap = np.full((t_len, ext), -1, np.int64)
    t = np.arange(t_len)[:, None]
    j = np.arange(lg + t_len)[None, :]
    dist = lg + t - j
    valid = (dist >= 0) & (dist % dil == 0) & (dist <= band * dil)
    bmap[:, :lg + t_len] = np.where(valid, _t5_bucket(np.maximum(dist, 0)), -1)
    return np.tile(bmap, (GROUP_HEADS, 1)).astype(np.int32)


def _attn_sample_side(g, tab, q16, kvnew, st, n_steps, step_of):
    window, dil = ATT_GROUPS[g]
    nb, _, lg = st.shape
    assert lg == window and nb % n_steps == 0
    per = nb // n_steps
    t_len = kvnew.shape[1]
    nrow = GROUP_HEADS * t_len
    bmap_np = _sample_bucket_map(window, dil, lg, t_len)
    buckets = tuple(int(b) for b in np.unique(bmap_np) if b >= 0)
    bmap = jnp.asarray(bmap_np)

    def blk(*idx):
        return (step_of(*idx), 0, 0)

    return _Side(
        name="attn_sample_g%d" % g,
        init=functools.partial(_attn_sample_init, g, t_len, buckets),
        step=functools.partial(_attn_sample_step, t_len),
        inputs=(tab, bmap, q16, kvnew, st),
        in_specs=(pl.BlockSpec(memory_space=pltpu.SMEM), _whole(bmap.shape),
                  pl.BlockSpec((per, nrow, GW), blk),
                  pl.BlockSpec((per, t_len, 2 * GW), blk),
                  pl.BlockSpec((per, 2 * GW, lg), blk)),
        out_shape=(jax.ShapeDtypeStruct((nb, 2 * GW, lg), F32),
                   jax.ShapeDtypeStruct((nb, t_len, 2 * GW), F32)),
        out_specs=(pl.BlockSpec((per, 2 * GW, lg), blk),
                   pl.BlockSpec((per, t_len, 2 * GW), blk)),
        scratch_shapes=(pltpu.VMEM((nrow, lg + BLK), F32), pltpu.VMEM((per, BLK, 2 * GW), F32),
                        pltpu.VMEM((per, 2 * GW, lg + BLK), BF16)))


def _mix_ffn_main(alpha, n_groups_in, in_refs, out_refs, scratch_refs):
    h_ref = in_refs[0]
    att_refs = in_refs[1:1 + n_groups_in]
    (yb_ref, sg_ref, woa_ref, wob_ref, wout_ref, g2_ref, b2_ref,
     w13_ref, w2_ref, g3_ref, b3_ref) = in_refs[1 + n_groups_in:]
    (o_ref,) = out_refs
    h2_ref, h2b_ref, acc_ref = scratch_refs
    if n_groups_in == 1:
        att = att_refs[0][...]
    else:
        att = _combine_groups([r[:, :GW] for r in att_refs], [r[:, GW:] for r in att_refs]).astype(BF16)
    a = jnp.dot(att, woa_ref[...], preferred_element_type=F32)
    bb = jnp.dot(yb_ref[...], wob_ref[...], preferred_element_type=F32)
    gated = sg_ref[:, :D_MODEL].astype(F32) * a + sg_ref[:, D_MODEL:].astype(F32) * bb
    mix = jnp.dot(gated.astype(BF16), wout_ref[...], preferred_element_type=F32)
    h2 = _ln(alpha * h_ref[...] + mix, g2_ref[...], b2_ref[...])
    h2_ref[...] = h2
    h2b_ref[...] = h2.astype(BF16)
    _swiglu_ln(alpha, h2_ref, h2b_ref, w13_ref, w2_ref, g3_ref, b3_ref, acc_ref, o_ref)


def _mix_ffn(h2d, atts, yb, sg, woa, wob, wout, g2, b2, w13, w2, g3, b3, alpha):
    n = h2d.shape[0]
    assert n % TM == 0

    def rows(a):
        return pl.BlockSpec((TM, a.shape[1]), lambda i: (i, 0))

    ins = [h2d, *atts, yb, sg, woa, wob, wout, g2, b2, w13, w2, g3, b3]
    in_specs = ([rows(a) for a in (h2d, *atts, yb, sg)]
                + [_whole(a.shape) for a in (woa, wob, wout, g2, b2)] + _ffn_specs())
    (out,), _ = _call(
        "mix_ffn", functools.partial(_mix_ffn_main, alpha, len(atts)), None, (n // TM,), ins, in_specs,
        [jax.ShapeDtypeStruct((n, D_MODEL), F32)], [rows(h2d)],
        [pltpu.VMEM((TM, D_MODEL), F32), pltpu.VMEM((TM, D_MODEL), BF16), pltpu.VMEM((TM, D_MODEL), F32)])
    return out


def _prep_ffn(w1, w3, w2):
    return (jnp.stack([w1, w3]).astype(BF16), w2.reshape(NC, FC, D_MODEL).astype(BF16))


def _prep_w_in(w_in, b_in):
    def cols(a, lo, n):
        return a[..., lo:lo + n]
    wq, bq = [], []
    for g in range(N_GROUPS):
        parts = [(i * ATT_W + g * GW, GW) for i in range(3)]
        wq.append(jnp.concatenate([cols(w_in, lo, n) for lo, n in parts], axis=-1))
        bq.append(jnp.concatenate([cols(b_in, lo, n) for lo, n in parts], axis=-1))
    o_uv = 3 * ATT_W
    return (jnp.stack(wq).astype(BF16), jnp.stack(bq)[:, None, :],
            cols(w_in, o_uv, 2 * SGU_W).astype(BF16), cols(b_in, o_uv, 2 * SGU_W)[None, :],
            cols(w_in, o_uv + 2 * SGU_W, 2 * D_MODEL).astype(BF16),
            cols(b_in, o_uv + 2 * SGU_W, 2 * D_MODEL)[None, :])


def _to_window(a):
    return a.reshape(a.shape[0], a.shape[1], 2, GROUP_HEADS, HEAD_DIM)


def kernel(x_prompt, x_sample, state_win0, state_win1, state_win2, rel_bias, ln1_g, ln1_b, f1_w1, f1_w3, f1_w2, w_in, b_in, sgu_ln_g, sgu_ln_b, sgu_ws, sgu_b, w_oa, w_ob, w_out, ln2_g, ln2_b, f2_w1, f2_w3, f2_w2, ln3_g, ln3_b):
    depth = ln1_g.shape[0]
    alpha = (2 * depth) ** 0.25
    batch, seq, _ = x_prompt.shape
    dec_batch, dec_seq, _ = x_sample.shape
    states = (state_win0, state_win1, state_win2)

    yp = x_prompt.reshape(batch * seq, D_MODEL)
    ys = x_sample.reshape(dec_batch * dec_seq, D_MODEL)
    outs = [[] for _ in range(7)]
    for l in range(depth):
        f1 = _prep_ffn(f1_w1[l], f1_w3[l], f1_w2[l])
        f2 = _prep_ffn(f2_w1[l], f2_w3[l], f2_w2[l])
        wqkv, bqkv, wuv, buv, wg, bg = _prep_w_in(w_in[l], b_in[l])
        sln = (sgu_ln_g[l][None, :], sgu_ln_b[l][None, :])
        woa, wob, wout = w_oa[l].astype(BF16), w_ob[l].astype(BF16), w_out[l].astype(BF16)
        ln1 = (ln1_g[l][None, :], ln1_b[l][None, :])
        ln2 = (ln2_g[l][None, :], ln2_b[l][None, :])
        ln3 = (ln3_g[l][None, :], ln3_b[l][None, :])

        hs, _ = _ffn_ln(ys, *f1, *ln1, alpha)
        ws4 = sgu_ws[l][:, :dec_seq, :dec_seq].reshape(SGU_GROUPS, dec_seq * dec_seq)
        sb4 = sgu_b[l][:, :dec_seq]
        qs, kvs, ybs, sgs, vn = _proj_sample(hs, dec_seq, ws4, sb4, wqkv, bqkv, wuv, buv, wg, bg, *sln)
        nt = seq // TM
        n_steps = batch * nt
        sides = []
        for g in range(N_GROUPS):
            st = states[g][l]
            lg = st.shape[1]
            st_cm = jnp.transpose(st, (0, 2, 3, 4, 1)).reshape(dec_batch, 2 * GW, lg)
            q16 = jnp.tile(qs[:, g * GW:(g + 1) * GW].reshape(dec_batch, 1, dec_seq, GW),
                           (1, GROUP_HEADS, 1, 1)).reshape(dec_batch, GROUP_HEADS * dec_seq, GW)
            kvnew = kvs[:, g * 2 * GW:(g + 1) * 2 * GW].reshape(dec_batch, dec_seq, 2 * GW)
            step_of = (lambda i: i) if g == N_GROUPS - 1 else (lambda b, t: b * nt + t)
            sides.append(_attn_sample_side(g, rel_bias, q16, kvnew, st_cm, n_steps, step_of))

        hp, side_a = _ffn_ln(yp, *f1, *ln1, alpha, sides=sides[2:])
        res, side_b = _proj_prompt(hp, batch, seq, wqkv, bqkv, wuv, buv, wg, bg, *sln,
                                   sgu_ws[l], jnp.transpose(sgu_b[l]), sides=sides[:2])
        qkvs, wins, ybp, sgp = res[0:3], res[3:6], res[6], res[7]
        att = _attn_prompt(rel_bias, qkvs, batch).reshape(batch * seq, GW)
        yp = _mix_ffn(hp, [att], ybp, sgp, woa, wob, wout, *ln2, *f2, *ln3, alpha)
        for g in range(N_GROUPS):
            outs[g].append(_to_window(wins[g]))

        ols_s = []
        for g, (win_cm, ol) in enumerate(side_b + side_a):
            lg = win_cm.shape[2]
            win = jnp.transpose(win_cm.reshape(dec_batch, 2, GROUP_HEADS, HEAD_DIM, lg), (0, 4, 1, 2, 3))
            outs[3 + g].append(win)
            ols_s.append(ol.reshape(dec_batch * dec_seq, 2 * GW))
        ys = _mix_ffn(hs, ols_s, ybs, sgs, woa, wob, wout, *ln2, *f2, *ln3, alpha)
        outs[6].append(vn.reshape(dec_batch, dec_seq, SGU_W))

    return (yp.reshape(batch, seq, D_MODEL), ys.reshape(dec_batch, dec_seq, D_MODEL),
            *[jnp.stack(o) for o in outs])
```

```python
import functools
import math
from typing import Callable, NamedTuple

import jax
import jax.numpy as jnp
import numpy as np
from jax import lax
from jax.experimental import pallas as pl
from jax.experimental.pallas import tpu as pltpu

F32 = jnp.float32
BF16 = jnp.bfloat16

D_MODEL = 1024
HEAD_DIM = 64
GROUP_HEADS = 4
ATT_GROUPS = ((128, 1), (512, 4), (2048, 16))
N_GROUPS = len(ATT_GROUPS)
N_ATT_HEADS = N_GROUPS * GROUP_HEADS
ATT_W = N_ATT_HEADS * HEAD_DIM
GW = GROUP_HEADS * HEAD_DIM
BLK = 128
N_BUCKETS = 32
MAX_DISTANCE = 2048
SGU_CHUNK = 128
SGU_GROUPS = 4
SGU_GROUP_W = 128
SGU_W = SGU_GROUPS * SGU_GROUP_W
D_FF = 2816
LN_EPS = 1e-5
NEG = -1e30
SCALE = HEAD_DIM ** -0.5

TM = 512
FC = 256
NC = D_FF // FC
ATTN_UNROLL = 16
ATTN_UNROLL_SINGLE = 4
VMEM_LIMIT = 60 * 1024 * 1024


def _t5_bucket(dist):
    dist = np.asarray(dist, np.int64)
    max_exact = N_BUCKETS // 2
    large = max_exact + (np.log(np.maximum(dist, max_exact) / max_exact)
                         / np.log(MAX_DISTANCE / max_exact) * (N_BUCKETS - max_exact)).astype(np.int64)
    large = np.minimum(large, N_BUCKETS - 1)
    return np.where(dist < max_exact, dist, large).astype(np.int32)


def _ln(t, g, b):
    mu = jnp.mean(t, axis=-1, keepdims=True)
    d = t - mu
    var = jnp.mean(d * d, axis=-1, keepdims=True)
    return d * lax.rsqrt(var + LN_EPS) * g + b


def _whole(shape):
    nd = len(shape)
    return pl.BlockSpec(shape, lambda *_: (0,) * nd, pipeline_mode=pl.Buffered(1))


class _Side(NamedTuple):
    name: str
    init: Callable
    step: Callable
    inputs: tuple
    in_specs: tuple
    out_shape: tuple
    out_specs: tuple
    scratch_shapes: tuple


def _take(refs, pos, count):
    return refs[pos:pos + count], pos + count


def _call(name, main, main_init, grid, inputs, in_specs, out_shape, out_specs, scratch, sides=()):
    n_in, n_out, n_scr = len(inputs), len(out_shape), len(scratch)

    def body(*refs):
        refs = list(refs)
        m_in, pos = _take(refs, 0, n_in)
        s_in = []
        for s in sides:
            r, pos = _take(refs, pos, len(s.inputs))
            s_in.append(r)
        m_out, pos = _take(refs, pos, n_out)
        s_out = []
        for s in sides:
            r, pos = _take(refs, pos, len(s.out_shape))
            s_out.append(r)
        m_scr, pos = _take(refs, pos, n_scr)
        s_scr = []
        for s in sides:
            r, pos = _take(refs, pos, len(s.scratch_shapes))
            s_scr.append(r)

        if main_init is not None or sides:
            first = pl.program_id(0) == 0
            for axis in range(1, len(grid)):
                first = jnp.logical_and(first, pl.program_id(axis) == 0)

            @pl.when(first)
            def _():
                if main_init is not None:
                    main_init(m_in, m_out, m_scr)
                for s, i, o, c in zip(sides, s_in, s_out, s_scr):
                    s.init(i, o, c)

        main(m_in, m_out, m_scr)
        for s, i, o, c in zip(sides, s_in, s_out, s_scr):
            s.step(i, o, c)

    all_in = list(inputs) + [a for s in sides for a in s.inputs]
    all_in_specs = list(in_specs) + [a for s in sides for a in s.in_specs]
    all_out_shape = list(out_shape) + [a for s in sides for a in s.out_shape]
    all_out_specs = list(out_specs) + [a for s in sides for a in s.out_specs]
    all_scratch = list(scratch) + [a for s in sides for a in s.scratch_shapes]
    res = pl.pallas_call(
        body,
        grid=grid,
        in_specs=all_in_specs,
        out_specs=all_out_specs,
        out_shape=all_out_shape,
        scratch_shapes=all_scratch,
        compiler_params=pltpu.CompilerParams(dimension_semantics=("arbitrary",) * len(grid),
                                             vmem_limit_bytes=VMEM_LIMIT),
        name=name + "".join("_" + s.name for s in sides),
    )(*all_in)
    res = list(res)
    main_res, pos = _take(res, 0, n_out)
    side_res = []
    for s in sides:
        r, pos = _take(res, pos, len(s.out_shape))
        side_res.append(r)
    return main_res, side_res


def _swiglu_ln(alpha, x_ref, xb_ref, w13_ref, w2_ref, g_ref, b_ref, acc_ref, o_ref):
    for c in range(NC):
        cols = slice(c * FC, (c + 1) * FC)
        a = jnp.dot(xb_ref[...], w13_ref[0, :, cols], preferred_element_type=F32)
        b = jnp.dot(xb_ref[...], w13_ref[1, :, cols], preferred_element_type=F32)
        hid = (a * jax.nn.sigmoid(a)) * b
        part = jnp.dot(hid.astype(BF16), w2_ref[c], preferred_element_type=F32)
        if c == 0:
            acc_ref[...] = part
        else:
            acc_ref[...] += part
    t = alpha * x_ref[...] + 0.5 * acc_ref[...]
    o_ref[...] = _ln(t, g_ref[...], b_ref[...])


def _ffn_specs():
    return [_whole((2, D_MODEL, D_FF)), _whole((NC, FC, D_MODEL)), _whole((1, D_MODEL)), _whole((1, D_MODEL))]


def _ffn_ln_main(alpha, in_refs, out_refs, scratch_refs):
    x_ref, w13_ref, w2_ref, g_ref, b_ref = in_refs
    xb_ref, acc_ref = scratch_refs
    xb_ref[...] = x_ref[...].astype(BF16)
    _swiglu_ln(alpha, x_ref, xb_ref, w13_ref, w2_ref, g_ref, b_ref, acc_ref, out_refs[0])


def _ffn_ln(x2d, w13, w2, g, b, alpha, sides=()):
    n = x2d.shape[0]
    assert n % TM == 0
    (out,), side_res = _call(
        "ffn_ln", functools.partial(_ffn_ln_main, alpha), None, (n // TM,),
        [x2d, w13, w2, g, b],
        [pl.BlockSpec((TM, D_MODEL), lambda i: (i, 0))] + _ffn_specs(),
        [jax.ShapeDtypeStruct((n, D_MODEL), F32)],
        [pl.BlockSpec((TM, D_MODEL), lambda i: (i, 0))],
        [pltpu.VMEM((TM, D_MODEL), BF16), pltpu.VMEM((TM, D_MODEL), F32)],
        sides)
    return out, side_res


def _gelu_exact(z):
    return 0.5 * z * (1.0 + lax.erf(z * (1.0 / math.sqrt(2.0))))


def _uv_branch(hb, wuv_ref, buv_ref, sg_ref, sb_ref):
    zuv = jnp.dot(hb, wuv_ref[...], preferred_element_type=F32) + buv_ref[...]
    zz = _gelu_exact(zuv)
    u = zz[:, :SGU_W]
    vn = _ln(zz[:, SGU_W:], sg_ref[...], sb_ref[...])
    return u, vn


def _gates(hb, wg_ref, bg_ref, sg_out_ref):
    zg = jnp.dot(hb, wg_ref[...], preferred_element_type=F32) + bg_ref[...]
    sg_out_ref[...] = jax.nn.sigmoid(zg).astype(BF16)


def _proj_prompt_main(in_refs, out_refs, scratch_refs):
    (h_ref, wqkv_ref, bqkv_ref, wuv_ref, buv_ref, wg_ref, bg_ref,
     sln_g_ref, sln_b_ref, ws_ref, sbt_ref) = in_refs
    qkv_refs, win_refs = out_refs[0:3], out_refs[3:6]
    yb_ref, sg_ref = out_refs[6:8]
    (zs_ref,) = scratch_refs
    hb = h_ref[...].astype(BF16)
    u, vn = _uv_branch(hb, wuv_ref, buv_ref, sln_g_ref, sln_b_ref)
    _gates(hb, wg_ref, bg_ref, sg_ref)
    for g, (window, dil) in enumerate(ATT_GROUPS):
        zg = jnp.dot(hb, wqkv_ref[g], preferred_element_type=F32) + bqkv_ref[g]
        keep = min(window, TM)
        win_refs[g][0] = zg[TM - keep:, GW:]
        zq = jnp.concatenate([zg[:, :GW] * SCALE, zg[:, GW:]], axis=1)
        if dil == 1:
            qkv_refs[g][0, 0] = zq.astype(BF16)
            continue
        ncol = 3 * GW // BLK
        for j in range(ncol):
            zs_ref[j] = zq[:, j * BLK:(j + 1) * BLK]
        rows = TM // dil
        for r in range(dil):
            sub = jnp.concatenate([zs_ref[j, pl.ds(r, rows, stride=dil), :] for j in range(ncol)], axis=1)
            qkv_refs[g][0, r] = sub.astype(BF16)

    vnb = vn.astype(BF16)
    row = lax.broadcasted_iota(jnp.int32, (SGU_CHUNK, SGU_CHUNK), 0)
    col = lax.broadcasted_iota(jnp.int32, (SGU_CHUNK, SGU_CHUNK), 1)
    for g in range(SGU_GROUPS):
        wm = jnp.where(row >= col, ws_ref[g], 0.0).astype(BF16)
        bias = jnp.broadcast_to(sbt_ref[:, g:g + 1], (SGU_CHUNK, SGU_GROUP_W))
        cs = slice(g * SGU_GROUP_W, (g + 1) * SGU_GROUP_W)
        for c in range(TM // SGU_CHUNK):
            rs = slice(c * SGU_CHUNK, (c + 1) * SGU_CHUNK)
            mixed = jnp.dot(wm, vnb[rs, cs], preferred_element_type=F32) + bias
            yb_ref[rs, cs] = (u[rs, cs] * mixed).astype(BF16)


def _proj_prompt(h2d, batch, seq, wqkv, bqkv, wuv, buv, wg, bg, sln_g, sln_b, ws, sbt, sides=()):
    nt = seq // TM
    assert seq % TM == 0 and TM == ATT_GROUPS[1][0] and TM >= ATT_GROUPS[0][0]
    ins = [h2d, wqkv, bqkv, wuv, buv, wg, bg, sln_g, sln_b, ws, sbt]
    in_specs = [pl.BlockSpec((TM, D_MODEL), lambda b, t: (b * nt + t, 0))] + [_whole(a.shape) for a in ins[1:]]
    out_shape, out_specs = [], []
    for window, dil in ATT_GROUPS:
        out_shape.append(jax.ShapeDtypeStruct((batch, dil, seq // dil, 3 * GW), BF16))
        out_specs.append(pl.BlockSpec((1, dil, TM // dil, 3 * GW), lambda b, t: (b, 0, t, 0)))
    for window, dil in ATT_GROUPS:
        keep = min(window, seq)
        out_shape.append(jax.ShapeDtypeStruct((batch, keep, 2 * GW), F32))
        if keep > TM:
            assert keep == seq
            out_specs.append(pl.BlockSpec((1, TM, 2 * GW), lambda b, t: (b, t, 0)))
        else:
            out_specs.append(pl.BlockSpec((1, keep, 2 * GW), lambda b, t: (b, 0, 0)))
    out_shape += [jax.ShapeDtypeStruct((batch * seq, SGU_W), BF16),
                  jax.ShapeDtypeStruct((batch * seq, 2 * D_MODEL), BF16)]
    out_specs += [pl.BlockSpec((TM, SGU_W), lambda b, t: (b * nt + t, 0)),
                  pl.BlockSpec((TM, 2 * D_MODEL), lambda b, t: (b * nt + t, 0))]
    return _call("proj_prompt", _proj_prompt_main, None, (batch, nt), ins, in_specs, out_shape, out_specs,
                 [pltpu.VMEM((3 * GW // BLK, TM, BLK), F32)], sides)


def _proj_sample_kernel(t_len, ws4_ref, sb4_ref, h_ref, wqkv_ref, bqkv_ref, wuv_ref, buv_ref,
                        wg_ref, bg_ref, sln_g_ref, sln_b_ref,
                        q_ref, kv_ref, yb_ref, sg_ref, vn_ref):
    n = h_ref.shape[0]
    hb = h_ref[...].astype(BF16)
    for g in range(N_GROUPS):
        zg = jnp.dot(hb, wqkv_ref[g], preferred_element_type=F32) + bqkv_ref[g]
        q_ref[:, g * GW:(g + 1) * GW] = zg[:, :GW] * SCALE
        kv_ref[:, g * 2 * GW:(g + 1) * 2 * GW] = zg[:, GW:]

    u, vn = _uv_branch(hb, wuv_ref, buv_ref, sln_g_ref, sln_b_ref)
    vn_ref[...] = vn
    p = lax.broadcasted_iota(jnp.int32, (n, SGU_GROUP_W), 0) & (t_len - 1)
    for g in range(SGU_GROUPS):
        cs = slice(g * SGU_GROUP_W, (g + 1) * SGU_GROUP_W)
        vg = vn[:, cs]
        acc = jnp.zeros((n, SGU_GROUP_W), F32)
        for pp in range(t_len):
            acc = jnp.where(p == pp, sb4_ref[g, pp], acc)
        for d in range(t_len):
            coef = jnp.zeros((n, SGU_GROUP_W), F32)
            for pp in range(d, t_len):
                coef = jnp.where(p == pp, ws4_ref[g, pp * t_len + (pp - d)], coef)
            shifted = vg if d == 0 else pltpu.roll(vg, d, axis=0)
            acc = acc + coef * shifted
        yb_ref[:, cs] = (u[:, cs] * acc).astype(BF16)

    _gates(hb, wg_ref, bg_ref, sg_ref)


def _proj_sample(h2d, t_len, ws4, sb4, wqkv, bqkv, wuv, buv, wg, bg, sln_g, sln_b):
    n = h2d.shape[0]
    assert t_len & (t_len - 1) == 0
    smem = pl.BlockSpec(memory_space=pltpu.SMEM)
    ins = (h2d, wqkv, bqkv, wuv, buv, wg, bg, sln_g, sln_b)
    out_shape = [jax.ShapeDtypeStruct((n, N_GROUPS * GW), F32),
                 jax.ShapeDtypeStruct((n, N_GROUPS * 2 * GW), F32),
                 jax.ShapeDtypeStruct((n, SGU_W), BF16),
                 jax.ShapeDtypeStruct((n, 2 * D_MODEL), BF16),
                 jax.ShapeDtypeStruct((n, SGU_W), F32)]
    return pl.pallas_call(
        functools.partial(_proj_sample_kernel, t_len),
        grid=(1,),
        in_specs=[smem, smem] + [_whole(a.shape) for a in ins],
        out_specs=[_whole(s.shape) for s in out_shape],
        out_shape=out_shape,
        compiler_params=pltpu.CompilerParams(dimension_semantics=("arbitrary",), vmem_limit_bytes=VMEM_LIMIT),
        name="proj_sample",
    )(ws4, sb4, *ins)


def _bias_mask(bmap, buckets, value_of_bucket):
    acc = jnp.full(bmap.shape, NEG, F32)
    for bkt in buckets:
        acc = jnp.where(bmap == bkt, value_of_bucket(bkt), acc)
    return acc


def _prompt_bucket_map(window, dil):
    band = window // dil
    rel = np.arange(BLK)[:, None] - np.arange(2 * BLK)[None, :] + BLK
    valid = (rel >= 0) & (rel <= band)
    return np.where(valid, _t5_bucket(np.clip(rel, 0, band) * dil), -1).astype(np.int32)


def _combine_groups(outs, lses):
    mx = jnp.maximum(jnp.maximum(lses[0], lses[1]), lses[2])
    es = [jnp.exp(ls - mx) for ls in lses]
    den = es[0] + es[1] + es[2]
    num = es[0] * outs[0] + es[1] * outs[1] + es[2] * outs[2]
    return num * (1.0 / den)


def _attn_prompt_kernel(buckets, tab_ref, bmap_ref, qkv0_ref, qkv1_ref, qkv2_ref,
                        ol0_ref, ol1_ref, ol2_ref, bm_ref):
    @pl.when(pl.program_id(0) == 0)
    def _():
        for g in range(N_GROUPS):
            bmap = bmap_ref[g]
            for h in range(GROUP_HEADS):
                rs = slice(h * BLK, (h + 1) * BLK)
                bm = _bias_mask(bmap, buckets[g], lambda bkt: tab_ref[bkt, g * GROUP_HEADS + h])
                bm_ref[2 * g, rs, :] = bm
                bm_ref[2 * g + 1, rs, :BLK] = bm[:, BLK:]
                bm_ref[2 * g + 1, rs, BLK:] = jnp.full((BLK, BLK), NEG, F32)

    lane_head = lax.broadcasted_iota(jnp.int32, (BLK, GW), 1) >> 6
    lane_lo = lax.broadcasted_iota(jnp.int32, (BLK, BLK), 1) < HEAD_DIM

    def unit(g, qkv_ref, ol_ref, r, n):
        single = qkv_ref.shape[2] == BLK
        static_n = isinstance(n, int)
        q0 = n * BLK if static_n else pl.multiple_of(n * BLK, BLK)
        q = qkv_ref[0, r, pl.ds(q0, BLK), 0:GW]
        qs = jnp.concatenate([jnp.where(lane_head == h, q, jnp.zeros_like(q))
                              for h in range(GROUP_HEADS)], axis=0)
        if single:
            k = qkv_ref[0, r, :, GW:2 * GW]
            v = qkv_ref[0, r, :, 2 * GW:3 * GW]
            bm = bm_ref[2 * g, :, BLK:]
        else:
            if static_n:
                k0 = max(q0 - BLK, 0)
                bm = bm_ref[2 * g + int(n == 0)]
            else:
                k0 = pl.multiple_of(jnp.maximum(q0 - BLK, 0), BLK)
                bm = bm_ref[2 * g + (n == 0).astype(jnp.int32)]
            k = qkv_ref[0, r, pl.ds(k0, 2 * BLK), GW:2 * GW]
            v = qkv_ref[0, r, pl.ds(k0, 2 * BLK), 2 * GW:3 * GW]
        s = lax.dot_general(qs, k, (((1,), (1,)), ((), ())), preferred_element_type=F32) + bm
        m = jnp.max(s, axis=-1, keepdims=True)
        p = jnp.exp(s - m)
        l = jnp.sum(p, axis=-1, keepdims=True)
        pb = p.astype(BF16)
        linv = 1.0 / l
        lse = m + jnp.log(l)
        rows = pl.ds(q0, BLK)
        for half in range(2):
            pv = jnp.dot(pb[2 * half * BLK:(2 * half + 2) * BLK], v[:, half * BLK:(half + 1) * BLK],
                         preferred_element_type=F32)
            ra = slice(2 * half * BLK, (2 * half + 1) * BLK)
            rb = slice((2 * half + 1) * BLK, (2 * half + 2) * BLK)
            ol_ref[0, r, rows, half * BLK:(half + 1) * BLK] = jnp.where(
                lane_lo, pv[:BLK] * linv[ra], pv[BLK:] * linv[rb])
            ol_ref[0, r, rows, GW + half * BLK:GW + (half + 1) * BLK] = jnp.where(lane_lo, lse[ra], lse[rb])

    for g, (qkv_ref, ol_ref) in enumerate(((qkv0_ref, ol0_ref), (qkv1_ref, ol1_ref), (qkv2_ref, ol2_ref))):
        dil, sub_len = qkv_ref.shape[1], qkv_ref.shape[2]
        nb = sub_len // BLK
        n_units = dil * nb
        unroll = ATTN_UNROLL_SINGLE if nb == 1 else ATTN_UNROLL
        assert n_units % unroll == 0 and (nb == 1 or unroll % nb == 0 or dil == 1)

        def trip(t, carry, g=g, qkv_ref=qkv_ref, ol_ref=ol_ref, nb=nb, dil=dil, unroll=unroll):
            for j in range(unroll):
                if nb == 1:
                    r, n = t * unroll + j, 0
                elif dil == 1:
                    r, n = 0, t * unroll + j
                else:
                    r, n = t * (unroll // nb) + j // nb, j % nb
                unit(g, qkv_ref, ol_ref, r, n)
            return carry

        lax.fori_loop(0, n_units // unroll, trip, 0)


def _attn_prompt(tab, qkvs, batch):
    bmaps = [_prompt_bucket_map(w, d) for w, d in ATT_GROUPS]
    buckets = tuple(tuple(int(b) for b in np.unique(m) if b >= 0) for m in bmaps)
    bmap = jnp.asarray(np.stack(bmaps))
    in_specs = [pl.BlockSpec(memory_space=pltpu.SMEM), _whole(bmap.shape)]
    out_shape, out_specs = [], []
    for a in qkvs:
        _, dil, sub_len, _ = a.shape
        in_specs.append(pl.BlockSpec((1, dil, sub_len, 3 * GW), lambda b: (b, 0, 0, 0)))
        out_shape.append(jax.ShapeDtypeStruct((batch, dil, sub_len, 2 * GW), F32))
        out_specs.append(pl.BlockSpec((1, dil, sub_len, 2 * GW), lambda b: (b, 0, 0, 0)))
    return pl.pallas_call(
        functools.partial(_attn_prompt_kernel, buckets),
        grid=(batch,),
        in_specs=in_specs,
        out_specs=out_specs,
        out_shape=out_shape,
        scratch_shapes=[pltpu.VMEM((2 * N_GROUPS, GROUP_HEADS * BLK, 2 * BLK), F32)],
        compiler_params=pltpu.CompilerParams(dimension_semantics=("arbitrary",), vmem_limit_bytes=VMEM_LIMIT),
        name="attn_prompt",
    )(tab, bmap, *qkvs)


def _attn_sample_init(g, t_len, buckets, in_refs, out_refs, scratch_refs):
    tab_ref, bmap_ref = in_refs[:2]
    bm_ref, tr_ref, _ = scratch_refs
    rh = lax.broadcasted_iota(jnp.int32, bm_ref.shape, 0) // t_len

    def value(bkt):
        v = jnp.full(bm_ref.shape, tab_ref[bkt, g * GROUP_HEADS], F32)
        for h in range(1, GROUP_HEADS):
            v = jnp.where(rh == h, tab_ref[bkt, g * GROUP_HEADS + h], v)
        return v

    bm_ref[...] = _bias_mask(bmap_ref[...], buckets, value)
    tr_ref[...] = jnp.zeros_like(tr_ref)


def _attn_sample_step(t_len, in_refs, out_refs, scratch_refs):
    _, _, q_ref, kvnew_ref, st_ref = in_refs
    win_ref, ol_ref = out_refs
    bm_ref, tr_ref, fb_ref = scratch_refs
    lg = st_ref.shape[2]
    nrow = GROUP_HEADS * t_len
    row_head = lax.broadcasted_iota(jnp.int32, (nrow, GW), 0) // t_len
    lane_head = lax.broadcasted_iota(jnp.int32, (nrow, GW), 1) >> 6
    ext = lg + BLK
    sel = row_head == lane_head
    for i in range(st_ref.shape[0]):
        tr_ref[i, 0:t_len, :] = kvnew_ref[i]
        for c in range(2 * GW // BLK):
            rows = slice(c * BLK, (c + 1) * BLK)
            full = jnp.concatenate([st_ref[i, rows, :], tr_ref[i, :, rows].T], axis=1)
            win_ref[i, rows, :] = pltpu.roll(full, ext - t_len, axis=1)[:, :lg]
            fb_ref[i, rows, :] = full.astype(BF16)

        qs = jnp.where(sel, q_ref[i], 0.0).astype(BF16)
        s = jnp.dot(qs, fb_ref[i, 0:GW, :], preferred_element_type=F32) + bm_ref[...]
        m = jnp.max(s, axis=-1, keepdims=True)
        p = jnp.exp(s - m)
        l = jnp.sum(p, axis=-1, keepdims=True)
        pv = lax.dot_general(p.astype(BF16), fb_ref[i, GW:2 * GW, :], (((1,), (1,)), ((), ())),
                             preferred_element_type=F32) * (1.0 / l)
        lse = jnp.broadcast_to(m + jnp.log(l), (nrow, GW))
        o = jnp.where(sel, pv, 0.0)
        ls = jnp.where(sel, lse, 0.0)
        osum, lsum = o, ls
        for h in range(1, GROUP_HEADS):
            osum = osum + pltpu.roll(o, h * t_len, axis=0)
            lsum = lsum + pltpu.roll(ls, h * t_len, axis=0)
        ol_ref[i, :, 0:GW] = osum[0:t_len]
        ol_ref[i, :, GW:2 * GW] = lsum[0:t_len]


def _sample_bucket_map(window, dil, lg, t_len):
    band = window // dil
    ext = lg + BLK
    bmap = np.full((t_len, ext), -1, np.int64)
    t = np.arange(t_len)[:, None]
    j = np.arange(lg + t_len)[None, :]
    dist = lg + t - j
    valid = (dist >= 0) & (dist % dil == 0) & (dist <= band * dil)
    bmap[:, :lg + t_len] = np.where(valid, _t5_bucket(np.maximum(dist, 0)), -1)
    return np.tile(bmap, (GROUP_HEADS, 1)).astype(np.int32)


def _attn_sample_side(g, tab, q16, kvnew, st, n_steps, step_of):
    window, dil = ATT_GROUPS[g]
    nb, _, lg = st.shape
    assert lg == window and nb % n_steps == 0
    per = nb // n_steps
    t_len = kvnew.shape[1]
    nrow = GROUP_HEADS * t_len
    bmap_np = _sample_bucket_map(window, dil, lg, t_len)
    buckets = tuple(int(b) for b in np.unique(bmap_np) if b >= 0)
    bmap = jnp.asarray(bmap_np)

    def blk(*idx):
        return (step_of(*idx), 0, 0)

    return _Side(
        name="attn_sample_g%d" % g,
        init=functools.partial(_attn_sample_init, g, t_len, buckets),
        step=functools.partial(_attn_sample_step, t_len),
        inputs=(tab, bmap, q16, kvnew, st),
        in_specs=(pl.BlockSpec(memory_space=pltpu.SMEM), _whole(bmap.shape),
                  pl.BlockSpec((per, nrow, GW), blk),
                  pl.BlockSpec((per, t_len, 2 * GW), blk),
                  pl.BlockSpec((per, 2 * GW, lg), blk)),
        out_shape=(jax.ShapeDtypeStruct((nb, 2 * GW, lg), F32),
                   jax.ShapeDtypeStruct((nb, t_len, 2 * GW), F32)),
        out_specs=(pl.BlockSpec((per, 2 * GW, lg), blk),
                   pl.BlockSpec((per, t_len, 2 * GW), blk)),
        scratch_shapes=(pltpu.VMEM((nrow, lg + BLK), F32), pltpu.VMEM((per, BLK, 2 * GW), F32),
                        pltpu.VMEM((per, 2 * GW, lg + BLK), BF16)))


def _mix_ffn_main(alpha, in_refs, out_refs, scratch_refs):
    h_ref = in_refs[0]
    ol_refs = in_refs[1:1 + N_GROUPS]
    (yb_ref, sg_ref, woa_ref, wob_ref, wout_ref, g2_ref, b2_ref,
     w13_ref, w2_ref, g3_ref, b3_ref) = in_refs[1 + N_GROUPS:]
    (o_ref,) = out_refs
    u_ref, h2_ref, h2b_ref, acc_ref = scratch_refs
    tm = h_ref.shape[0]
    ncol = 2 * GW // BLK
    ols = []
    for gi, ol_ref in enumerate(ol_refs):
        dil = ol_ref.shape[1]
        if dil == 1:
            ols.append(ol_ref[0, 0])
            continue
        for r in range(dil):
            for j in range(ncol):
                u_ref[gi, j, pl.ds(r, tm // dil, stride=dil), :] = ol_ref[0, r, :, j * BLK:(j + 1) * BLK]
        ols.append(jnp.concatenate([u_ref[gi, j] for j in range(ncol)], axis=1))
    att = _combine_groups([ol[:, :GW] for ol in ols], [ol[:, GW:] for ol in ols]).astype(BF16)
    a = jnp.dot(att, woa_ref[...], preferred_element_type=F32)
    bb = jnp.dot(yb_ref[...], wob_ref[...], preferred_element_type=F32)
    gated = sg_ref[:, :D_MODEL].astype(F32) * a + sg_ref[:, D_MODEL:].astype(F32) * bb
    mix = jnp.dot(gated.astype(BF16), wout_ref[...], preferred_element_type=F32)
    h2 = _ln(alpha * h_ref[...] + mix, g2_ref[...], b2_ref[...])
    h2_ref[...] = h2
    h2b_ref[...] = h2.astype(BF16)
    _swiglu_ln(alpha, h2_ref, h2b_ref, w13_ref, w2_ref, g3_ref, b3_ref, acc_ref, o_ref)


def _mix_ffn(h2d, ols, yb, sg, woa, wob, wout, g2, b2, w13, w2, g3, b3, alpha, seq):
    n = h2d.shape[0]
    assert n % TM == 0 and seq % TM == 0
    nt = seq // TM

    def rows(a):
        return pl.BlockSpec((TM, a.shape[1]), lambda i: (i, 0))

    ins = [h2d, *ols, yb, sg, woa, wob, wout, g2, b2, w13, w2, g3, b3]
    in_specs = [rows(h2d)]
    for a in ols:
        dil = a.shape[1]
        in_specs.append(pl.BlockSpec((1, dil, TM // dil, 2 * GW), lambda i: (i // nt, 0, i % nt, 0)))
    in_specs += ([rows(yb), rows(sg)] + [_whole(a.shape) for a in (woa, wob, wout, g2, b2)] + _ffn_specs())
    (out,), _ = _call(
        "mix_ffn", functools.partial(_mix_ffn_main, alpha), None, (n // TM,), ins, in_specs,
        [jax.ShapeDtypeStruct((n, D_MODEL), F32)], [rows(h2d)],
        [pltpu.VMEM((N_GROUPS, 2 * GW // BLK, TM, BLK), F32), pltpu.VMEM((TM, D_MODEL), F32),
         pltpu.VMEM((TM, D_MODEL), BF16), pltpu.VMEM((TM, D_MODEL), F32)])
    return out


def _prep_ffn(w1, w3, w2):
    return (jnp.stack([w1, w3]).astype(BF16), w2.reshape(NC, FC, D_MODEL).astype(BF16))


def _prep_w_in(w_in, b_in):
    def cols(a, lo, n):
        return a[..., lo:lo + n]
    wq, bq = [], []
    for g in range(N_GROUPS):
        parts = [(i * ATT_W + g * GW, GW) for i in range(3)]
        wq.append(jnp.concatenate([cols(w_in, lo, n) for lo, n in parts], axis=-1))
        bq.append(jnp.concatenate([cols(b_in, lo, n) for lo, n in parts], axis=-1))
    o_uv = 3 * ATT_W
    return (jnp.stack(wq).astype(BF16), jnp.stack(bq)[:, None, :],
            cols(w_in, o_uv, 2 * SGU_W).astype(BF16), cols(b_in, o_uv, 2 * SGU_W)[None, :],
            cols(w_in, o_uv + 2 * SGU_W, 2 * D_MODEL).astype(BF16),
            cols(b_in, o_uv + 2 * SGU_W, 2 * D_MODEL)[None, :])


def _to_window(a):
    return a.reshape(a.shape[0], a.shape[1], 2, GROUP_HEADS, HEAD_DIM)


def kernel(x_prompt, x_sample, state_win0, state_win1, state_win2, rel_bias, ln1_g, ln1_b, f1_w1, f1_w3, f1_w2, w_in, b_in, sgu_ln_g, sgu_ln_b, sgu_ws, sgu_b, w_oa, w_ob, w_out, ln2_g, ln2_b, f2_w1, f2_w3, f2_w2, ln3_g, ln3_b):
    depth = ln1_g.shape[0]
    alpha = (2 * depth) ** 0.25
    batch, seq, _ = x_prompt.shape
    dec_batch, dec_seq, _ = x_sample.shape
    states = (state_win0, state_win1, state_win2)

    yp = x_prompt.reshape(batch * seq, D_MODEL)
    ys = x_sample.reshape(dec_batch * dec_seq, D_MODEL)
    outs = [[] for _ in range(7)]
    for l in range(depth):
        f1 = _prep_ffn(f1_w1[l], f1_w3[l], f1_w2[l])
        f2 = _prep_ffn(f2_w1[l], f2_w3[l], f2_w2[l])
        wqkv, bqkv, wuv, buv, wg, bg = _prep_w_in(w_in[l], b_in[l])
        sln = (sgu_ln_g[l][None, :], sgu_ln_b[l][None, :])
        woa, wob, wout = w_oa[l].astype(BF16), w_ob[l].astype(BF16), w_out[l].astype(BF16)
        ln1 = (ln1_g[l][None, :], ln1_b[l][None, :])
        ln2 = (ln2_g[l][None, :], ln2_b[l][None, :])
        ln3 = (ln3_g[l][None, :], ln3_b[l][None, :])

        hs, _ = _ffn_ln(ys, *f1, *ln1, alpha)
        ws4 = sgu_ws[l][:, :dec_seq, :dec_seq].reshape(SGU_GROUPS, dec_seq * dec_seq)
        sb4 = sgu_b[l][:, :dec_seq]
        qs, kvs, ybs, sgs, vn = _proj_sample(hs, dec_seq, ws4, sb4, wqkv, bqkv, wuv, buv, wg, bg, *sln)
        nt = seq // TM
        n_steps = batch * nt
        sides = []
        for g in range(N_GROUPS):
            st = states[g][l]
            lg = st.shape[1]
            st_cm = jnp.transpose(st, (0, 2, 3, 4, 1)).reshape(dec_batch, 2 * GW, lg)
            q16 = jnp.tile(qs[:, g * GW:(g + 1) * GW].reshape(dec_batch, 1, dec_seq, GW),
                           (1, GROUP_HEADS, 1, 1)).reshape(dec_batch, GROUP_HEADS * dec_seq, GW)
            kvnew = kvs[:, g * 2 * GW:(g + 1) * 2 * GW].reshape(dec_batch, dec_seq, 2 * GW)
            step_of = (lambda i: i) if g == N_GROUPS - 1 else (lambda b, t: b * nt + t)
            sides.append(_attn_sample_side(g, rel_bias, q16, kvnew, st_cm, n_steps, step_of))

        hp, side_a = _ffn_ln(yp, *f1, *ln1, alpha, sides=sides[2:])
        res, side_b = _proj_prompt(hp, batch, seq, wqkv, bqkv, wuv, buv, wg, bg, *sln,
                                   sgu_ws[l], jnp.transpose(sgu_b[l]), sides=sides[:2])
        qkvs, wins, ybp, sgp = res[0:3], res[3:6], res[6], res[7]
        ols = _attn_prompt(rel_bias, qkvs, batch)
        yp = _mix_ffn(hp, ols, ybp, sgp, woa, wob, wout, *ln2, *f2, *ln3, alpha, seq)
        for g in range(N_GROUPS):
            outs[g].append(_to_window(wins[g]))

        ols_s = []
        for g, (win_cm, ol) in enumerate(side_b + side_a):
            lg = win_cm.shape[2]
            win = jnp.transpose(win_cm.reshape(dec_batch, 2, GROUP_HEADS, HEAD_DIM, lg), (0, 4, 1, 2, 3))
            outs[3 + g].append(win)
            ols_s.append(ol.reshape(1, 1, dec_batch * dec_seq, 2 * GW))
        ys = _mix_ffn(hs, ols_s, ybs, sgs, woa, wob, wout, *ln2, *f2, *ln3, alpha, dec_batch * dec_seq)
        outs[6].append(vn.reshape(dec_batch, dec_seq, SGU_W))

    return (yp.reshape(batch, seq, D_MODEL), ys.reshape(dec_batch, dec_seq, D_MODEL),
            *[jnp.stack(o) for o in outs])
```

```python
import functools
import math
from typing import Callable, NamedTuple

import jax
import jax.numpy as jnp
import numpy as np
from jax import lax
from jax.experimental import pallas as pl
from jax.experimental.pallas import tpu as pltpu

F32 = jnp.float32
BF16 = jnp.bfloat16

D_MODEL = 1024
HEAD_DIM = 64
GROUP_HEADS = 4
ATT_GROUPS = ((128, 1), (512, 4), (2048, 16))
N_GROUPS = len(ATT_GROUPS)
N_ATT_HEADS = N_GROUPS * GROUP_HEADS
ATT_W = N_ATT_HEADS * HEAD_DIM
GW = GROUP_HEADS * HEAD_DIM
BLK = 128
N_BUCKETS = 32
MAX_DISTANCE = 2048
SGU_CHUNK = 128
SGU_GROUPS = 4
SGU_GROUP_W = 128
SGU_W = SGU_GROUPS * SGU_GROUP_W
D_FF = 2816
LN_EPS = 1e-5
NEG = -1e30
SCALE = HEAD_DIM ** -0.5

TM = 512
FC = 256
NC = D_FF // FC
ATTN_UNROLL = 16
ATTN_UNROLL_SINGLE = 4
VMEM_LIMIT = 60 * 1024 * 1024


def _t5_bucket(dist):
    dist = np.asarray(dist, np.int64)
    max_exact = N_BUCKETS // 2
    large = max_exact + (np.log(np.maximum(dist, max_exact) / max_exact)
                         / np.log(MAX_DISTANCE / max_exact) * (N_BUCKETS - max_exact)).astype(np.int64)
    large = np.minimum(large, N_BUCKETS - 1)
    return np.where(dist < max_exact, dist, large).astype(np.int32)


def _ln(t, g, b):
    mu = jnp.mean(t, axis=-1, keepdims=True)
    d = t - mu
    var = jnp.mean(d * d, axis=-1, keepdims=True)
    return d * lax.rsqrt(var + LN_EPS) * g + b


def _whole(shape):
    nd = len(shape)
    return pl.BlockSpec(shape, lambda *_: (0,) * nd, pipeline_mode=pl.Buffered(1))


class _Side(NamedTuple):
    name: str
    init: Callable
    step: Callable
    inputs: tuple
    in_specs: tuple
    out_shape: tuple
    out_specs: tuple
    scratch_shapes: tuple


def _take(refs, pos, count):
    return refs[pos:pos + count], pos + count


def _call(name, main, main_init, grid, inputs, in_specs, out_shape, out_specs, scratch, sides=()):
    n_in, n_out, n_scr = len(inputs), len(out_shape), len(scratch)

    def body(*refs):
        refs = list(refs)
        m_in, pos = _take(refs, 0, n_in)
        s_in = []
        for s in sides:
            r, pos = _take(refs, pos, len(s.inputs))
            s_in.append(r)
        m_out, pos = _take(refs, pos, n_out)
        s_out = []
        for s in sides:
            r, pos = _take(refs, pos, len(s.out_shape))
            s_out.append(r)
        m_scr, pos = _take(refs, pos, n_scr)
        s_scr = []
        for s in sides:
            r, pos = _take(refs, pos, len(s.scratch_shapes))
            s_scr.append(r)

        if main_init is not None or sides:
            first = pl.program_id(0) == 0
            for axis in range(1, len(grid)):
                first = jnp.logical_and(first, pl.program_id(axis) == 0)

            @pl.when(first)
            def _():
                if main_init is not None:
                    main_init(m_in, m_out, m_scr)
                for s, i, o, c in zip(sides, s_in, s_out, s_scr):
                    s.init(i, o, c)

        main(m_in, m_out, m_scr)
        for s, i, o, c in zip(sides, s_in, s_out, s_scr):
            s.step(i, o, c)

    all_in = list(inputs) + [a for s in sides for a in s.inputs]
    all_in_specs = list(in_specs) + [a for s in sides for a in s.in_specs]
    all_out_shape = list(out_shape) + [a for s in sides for a in s.out_shape]
    all_out_specs = list(out_specs) + [a for s in sides for a in s.out_specs]
    all_scratch = list(scratch) + [a for s in sides for a in s.scratch_shapes]
    res = pl.pallas_call(
        body,
        grid=grid,
        in_specs=all_in_specs,
        out_specs=all_out_specs,
        out_shape=all_out_shape,
        scratch_shapes=all_scratch,
        compiler_params=pltpu.CompilerParams(dimension_semantics=("arbitrary",) * len(grid),
                                             vmem_limit_bytes=VMEM_LIMIT),
        name=name + "".join("_" + s.name for s in sides),
    )(*all_in)
    res = list(res)
    main_res, pos = _take(res, 0, n_out)
    side_res = []
    for s in sides:
        r, pos = _take(res, pos, len(s.out_shape))
        side_res.append(r)
    return main_res, side_res


def _swiglu_ln(alpha, x_ref, xb_ref, w_refs, acc_ref, o_ref):
    w1_ref, w3_ref, w2_ref, g_ref, b_ref = w_refs
    for c in range(NC):
        cols = slice(c * FC, (c + 1) * FC)
        a = jnp.dot(xb_ref[...], w1_ref[:, cols], preferred_element_type=F32)
        b = jnp.dot(xb_ref[...], w3_ref[:, cols], preferred_element_type=F32)
        hid = (a * jax.nn.sigmoid(a)) * b
        part = jnp.dot(hid.astype(BF16), w2_ref[c], preferred_element_type=F32)
        if c == 0:
            acc_ref[...] = part
        else:
            acc_ref[...] += part
    t = alpha * x_ref[...] + 0.5 * acc_ref[...]
    o_ref[...] = _ln(t, g_ref[...], b_ref[...])


N_FFN_W = 5


def _ffn_specs():
    return [_whole((D_MODEL, D_FF)), _whole((D_MODEL, D_FF)), _whole((NC, FC, D_MODEL)),
            _whole((1, D_MODEL)), _whole((1, D_MODEL))]


def _ffn_ln_main(alpha, in_refs, out_refs, scratch_refs):
    x_ref = in_refs[0]
    xb_ref, acc_ref = scratch_refs
    xb_ref[...] = x_ref[...].astype(BF16)
    _swiglu_ln(alpha, x_ref, xb_ref, in_refs[1:1 + N_FFN_W], acc_ref, out_refs[0])


def _ffn_ln(x2d, ffn_w, alpha, sides=()):
    n = x2d.shape[0]
    assert n % TM == 0 and len(ffn_w) == N_FFN_W
    (out,), side_res = _call(
        "ffn_ln", functools.partial(_ffn_ln_main, alpha), None, (n // TM,),
        [x2d, *ffn_w],
        [pl.BlockSpec((TM, D_MODEL), lambda i: (i, 0))] + _ffn_specs(),
        [jax.ShapeDtypeStruct((n, D_MODEL), F32)],
        [pl.BlockSpec((TM, D_MODEL), lambda i: (i, 0))],
        [pltpu.VMEM((TM, D_MODEL), BF16), pltpu.VMEM((TM, D_MODEL), F32)],
        sides)
    return out, side_res


def _gelu_exact(z):
    return 0.5 * z * (1.0 + lax.erf(z * (1.0 / math.sqrt(2.0))))


def _uv_branch(hb, wuv_ref, buv_ref, sg_ref, sb_ref):
    zuv = jnp.dot(hb, wuv_ref[...], preferred_element_type=F32) + buv_ref[...]
    zz = _gelu_exact(zuv)
    u = zz[:, :SGU_W]
    vn = _ln(zz[:, SGU_W:], sg_ref[...], sb_ref[...])
    return u, vn


def _gates(hb, wg_ref, bg_ref, sg_out_ref):
    zg = jnp.dot(hb, wg_ref[...], preferred_element_type=F32) + bg_ref[...]
    sg_out_ref[...] = jax.nn.sigmoid(zg).astype(BF16)


def _proj_prompt_main(in_refs, out_refs, scratch_refs):
    (h_ref, wqkv_ref, bqkv_ref, wuv_ref, buv_ref, wg_ref, bg_ref,
     sln_g_ref, sln_b_ref, ws_ref, sbt_ref) = in_refs
    qkv_refs, win_refs = out_refs[0:3], out_refs[3:6]
    yb_ref, sg_ref = out_refs[6:8]
    zs_ref, hb_ref = scratch_refs
    hb_ref[...] = h_ref[...].astype(BF16)
    u, vn = _uv_branch(hb_ref[...], wuv_ref, buv_ref, sln_g_ref, sln_b_ref)
    _gates(hb_ref[...], wg_ref, bg_ref, sg_ref)

    vnb = vn.astype(BF16)
    row = lax.broadcasted_iota(jnp.int32, (SGU_CHUNK, SGU_CHUNK), 0)
    col = lax.broadcasted_iota(jnp.int32, (SGU_CHUNK, SGU_CHUNK), 1)
    for g in range(SGU_GROUPS):
        wm = jnp.where(row >= col, ws_ref[g], 0.0).astype(BF16)
        bias = jnp.broadcast_to(sbt_ref[:, g:g + 1], (SGU_CHUNK, SGU_GROUP_W))
        cs = slice(g * SGU_GROUP_W, (g + 1) * SGU_GROUP_W)
        for c in range(TM // SGU_CHUNK):
            rs = slice(c * SGU_CHUNK, (c + 1) * SGU_CHUNK)
            mixed = jnp.dot(wm, vnb[rs, cs], preferred_element_type=F32) + bias
            yb_ref[rs, cs] = (u[rs, cs] * mixed).astype(BF16)

    for g, (window, dil) in enumerate(ATT_GROUPS):
        zg = jnp.dot(hb_ref[...], wqkv_ref[g], preferred_element_type=F32) + bqkv_ref[g]
        keep = min(window, TM)
        win_refs[g][0] = zg[TM - keep:, GW:]
        zq = jnp.concatenate([zg[:, :GW] * SCALE, zg[:, GW:]], axis=1)
        if dil == 1:
            qkv_refs[g][0, 0] = zq.astype(BF16)
            continue
        ncol = 3 * GW // BLK
        for j in range(ncol):
            zs_ref[j] = zq[:, j * BLK:(j + 1) * BLK]
        rows = TM // dil
        for r in range(dil):
            sub = jnp.concatenate([zs_ref[j, pl.ds(r, rows, stride=dil), :] for j in range(ncol)], axis=1)
            qkv_refs[g][0, r] = sub.astype(BF16)


def _proj_prompt(h2d, batch, seq, wqkv, bqkv, wuv, buv, wg, bg, sln_g, sln_b, ws, sbt, sides=()):
    nt = seq // TM
    assert seq % TM == 0 and TM == ATT_GROUPS[1][0] and TM >= ATT_GROUPS[0][0]
    ins = [h2d, wqkv, bqkv, wuv, buv, wg, bg, sln_g, sln_b, ws, sbt]
    in_specs = [pl.BlockSpec((TM, D_MODEL), lambda b, t: (b * nt + t, 0))] + [_whole(a.shape) for a in ins[1:]]
    out_shape, out_specs = [], []
    for window, dil in ATT_GROUPS:
        out_shape.append(jax.ShapeDtypeStruct((batch, dil, seq // dil, 3 * GW), BF16))
        out_specs.append(pl.BlockSpec((1, dil, TM // dil, 3 * GW), lambda b, t: (b, 0, t, 0)))
    for window, dil in ATT_GROUPS:
        keep = min(window, seq)
        out_shape.append(jax.ShapeDtypeStruct((batch, keep, 2 * GW), F32))
        if keep > TM:
            assert keep == seq
            out_specs.append(pl.BlockSpec((1, TM, 2 * GW), lambda b, t: (b, t, 0)))
        else:
            out_specs.append(pl.BlockSpec((1, keep, 2 * GW), lambda b, t: (b, 0, 0)))
    out_shape += [jax.ShapeDtypeStruct((batch * seq, SGU_W), BF16),
                  jax.ShapeDtypeStruct((batch * seq, 2 * D_MODEL), BF16)]
    out_specs += [pl.BlockSpec((TM, SGU_W), lambda b, t: (b * nt + t, 0)),
                  pl.BlockSpec((TM, 2 * D_MODEL), lambda b, t: (b * nt + t, 0))]
    return _call("proj_prompt", _proj_prompt_main, None, (batch, nt), ins, in_specs, out_shape, out_specs,
                 [pltpu.VMEM((3 * GW // BLK, TM, BLK), F32), pltpu.VMEM((TM, D_MODEL), BF16)], sides)


def _proj_sample_kernel(t_len, ws4_ref, sb4_ref, h_ref, wqkv_ref, bqkv_ref, wuv_ref, buv_ref,
                        wg_ref, bg_ref, sln_g_ref, sln_b_ref,
                        q_ref, kv_ref, yb_ref, sg_ref, vn_ref):
    n = h_ref.shape[0]
    hb = h_ref[...].astype(BF16)
    for g in range(N_GROUPS):
        zg = jnp.dot(hb, wqkv_ref[g], preferred_element_type=F32) + bqkv_ref[g]
        q_ref[:, g * GW:(g + 1) * GW] = zg[:, :GW] * SCALE
        kv_ref[:, g * 2 * GW:(g + 1) * 2 * GW] = zg[:, GW:]

    u, vn = _uv_branch(hb, wuv_ref, buv_ref, sln_g_ref, sln_b_ref)
    vn_ref[...] = vn
    p = lax.broadcasted_iota(jnp.int32, (n, SGU_GROUP_W), 0) & (t_len - 1)
    for g in range(SGU_GROUPS):
        cs = slice(g * SGU_GROUP_W, (g + 1) * SGU_GROUP_W)
        vg = vn[:, cs]
        acc = jnp.zeros((n, SGU_GROUP_W), F32)
        for pp in range(t_len):
            acc = jnp.where(p == pp, sb4_ref[g, pp], acc)
        for d in range(t_len):
            coef = jnp.zeros((n, SGU_GROUP_W), F32)
            for pp in range(d, t_len):
                coef = jnp.where(p == pp, ws4_ref[g, pp * t_len + (pp - d)], coef)
            shifted = vg if d == 0 else pltpu.roll(vg, d, axis=0)
            acc = acc + coef * shifted
        yb_ref[:, cs] = (u[:, cs] * acc).astype(BF16)

    _gates(hb, wg_ref, bg_ref, sg_ref)


def _proj_sample(h2d, t_len, ws4, sb4, wqkv, bqkv, wuv, buv, wg, bg, sln_g, sln_b):
    n = h2d.shape[0]
    assert t_len & (t_len - 1) == 0
    smem = pl.BlockSpec(memory_space=pltpu.SMEM)
    ins = (h2d, wqkv, bqkv, wuv, buv, wg, bg, sln_g, sln_b)
    out_shape = [jax.ShapeDtypeStruct((n, N_GROUPS * GW), F32),
                 jax.ShapeDtypeStruct((n, N_GROUPS * 2 * GW), F32),
                 jax.ShapeDtypeStruct((n, SGU_W), BF16),
                 jax.ShapeDtypeStruct((n, 2 * D_MODEL), BF16),
                 jax.ShapeDtypeStruct((n, SGU_W), F32)]
    return pl.pallas_call(
        functools.partial(_proj_sample_kernel, t_len),
        grid=(1,),
        in_specs=[smem, smem] + [_whole(a.shape) for a in ins],
        out_specs=[_whole(s.shape) for s in out_shape],
        out_shape=out_shape,
        compiler_params=pltpu.CompilerParams(dimension_semantics=("arbitrary",), vmem_limit_bytes=VMEM_LIMIT),
        name="proj_sample",
    )(ws4, sb4, *ins)


def _bias_mask(bmap, buckets, value_of_bucket):
    acc = jnp.full(bmap.shape, NEG, F32)
    for bkt in buckets:
        acc = jnp.where(bmap == bkt, value_of_bucket(bkt), acc)
    return acc


def _prompt_bucket_map(window, dil):
    band = window // dil
    rel = np.arange(BLK)[:, None] - np.arange(2 * BLK)[None, :] + BLK
    valid = (rel >= 0) & (rel <= band)
    return np.where(valid, _t5_bucket(np.clip(rel, 0, band) * dil), -1).astype(np.int32)


def _combine_groups(outs, lses):
    mx = jnp.maximum(jnp.maximum(lses[0], lses[1]), lses[2])
    es = [jnp.exp(ls - mx) for ls in lses]
    den = es[0] + es[1] + es[2]
    num = es[0] * outs[0] + es[1] * outs[1] + es[2] * outs[2]
    return num * (1.0 / den)


def _attn_prompt_kernel(buckets, tab_ref, bmap_ref, qkv0_ref, qkv1_ref, qkv2_ref,
                        ol0_ref, ol1_ref, ol2_ref, bm_ref):
    @pl.when(pl.program_id(0) == 0)
    def _():
        for g in range(N_GROUPS):
            bmap = bmap_ref[g]
            for h in range(GROUP_HEADS):
                rs = slice(h * BLK, (h + 1) * BLK)
                bm = _bias_mask(bmap, buckets[g], lambda bkt: tab_ref[bkt, g * GROUP_HEADS + h])
                bm_ref[2 * g, rs, :] = bm
                bm_ref[2 * g + 1, rs, :BLK] = bm[:, BLK:]
                bm_ref[2 * g + 1, rs, BLK:] = jnp.full((BLK, BLK), NEG, F32)

    lane_head = lax.broadcasted_iota(jnp.int32, (BLK, GW), 1) >> 6
    lane_lo = lax.broadcasted_iota(jnp.int32, (BLK, BLK), 1) < HEAD_DIM

    def unit(g, qkv_ref, ol_ref, r, n):
        single = qkv_ref.shape[2] == BLK
        static_n = isinstance(n, int)
        q0 = n * BLK if static_n else pl.multiple_of(n * BLK, BLK)
        q = qkv_ref[0, r, pl.ds(q0, BLK), 0:GW]
        qs = jnp.concatenate([jnp.where(lane_head == h, q, jnp.zeros_like(q))
                              for h in range(GROUP_HEADS)], axis=0)
        if single:
            k = qkv_ref[0, r, :, GW:2 * GW]
            v = qkv_ref[0, r, :, 2 * GW:3 * GW]
            bm = bm_ref[2 * g, :, BLK:]
        else:
            if static_n:
                k0 = max(q0 - BLK, 0)
                bm = bm_ref[2 * g + int(n == 0)]
            else:
                k0 = pl.multiple_of(jnp.maximum(q0 - BLK, 0), BLK)
                bm = bm_ref[2 * g + (n == 0).astype(jnp.int32)]
            k = qkv_ref[0, r, pl.ds(k0, 2 * BLK), GW:2 * GW]
            v = qkv_ref[0, r, pl.ds(k0, 2 * BLK), 2 * GW:3 * GW]
        s = lax.dot_general(qs, k, (((1,), (1,)), ((), ())), preferred_element_type=F32) + bm
        m = jnp.max(s, axis=-1, keepdims=True)
        p = jnp.exp(s - m)
        l = jnp.sum(p, axis=-1, keepdims=True)
        pb = p.astype(BF16)
        linv = 1.0 / l
        lse = m + jnp.log(l)
        rows = pl.ds(q0, BLK)
        for half in range(2):
            pv = jnp.dot(pb[2 * half * BLK:(2 * half + 2) * BLK], v[:, half * BLK:(half + 1) * BLK],
                         preferred_element_type=F32)
            ra = slice(2 * half * BLK, (2 * half + 1) * BLK)
            rb = slice((2 * half + 1) * BLK, (2 * half + 2) * BLK)
            ol_ref[0, r, rows, half * BLK:(half + 1) * BLK] = jnp.where(
                lane_lo, pv[:BLK] * linv[ra], pv[BLK:] * linv[rb])
            ol_ref[0, r, rows, GW + half * BLK:GW + (half + 1) * BLK] = jnp.where(lane_lo, lse[ra], lse[rb])

    for g, (qkv_ref, ol_ref) in enumerate(((qkv0_ref, ol0_ref), (qkv1_ref, ol1_ref), (qkv2_ref, ol2_ref))):
        dil, sub_len = qkv_ref.shape[1], qkv_ref.shape[2]
        nb = sub_len // BLK
        n_units = dil * nb
        unroll = ATTN_UNROLL_SINGLE if nb == 1 else ATTN_UNROLL
        assert n_units % unroll == 0 and (nb == 1 or unroll % nb == 0 or dil == 1)

        def trip(t, carry, g=g, qkv_ref=qkv_ref, ol_ref=ol_ref, nb=nb, dil=dil, unroll=unroll):
            for j in range(unroll):
                if nb == 1:
                    r, n = t * unroll + j, 0
                elif dil == 1:
                    r, n = 0, t * unroll + j
                else:
                    r, n = t * (unroll // nb) + j // nb, j % nb
                unit(g, qkv_ref, ol_ref, r, n)
            return carry

        lax.fori_loop(0, n_units // unroll, trip, 0)


def _attn_prompt(tab, qkvs, batch):
    bmaps = [_prompt_bucket_map(w, d) for w, d in ATT_GROUPS]
    buckets = tuple(tuple(int(b) for b in np.unique(m) if b >= 0) for m in bmaps)
    bmap = jnp.asarray(np.stack(bmaps))
    in_specs = [pl.BlockSpec(memory_space=pltpu.SMEM), _whole(bmap.shape)]
    out_shape, out_specs = [], []
    for a in qkvs:
        _, dil, sub_len, _ = a.shape
        in_specs.append(pl.BlockSpec((1, dil, sub_len, 3 * GW), lambda b: (b, 0, 0, 0)))
        out_shape.append(jax.ShapeDtypeStruct((batch, dil, sub_len, 2 * GW), F32))
        out_specs.append(pl.BlockSpec((1, dil, sub_len, 2 * GW), lambda b: (b, 0, 0, 0)))
    return pl.pallas_call(
        functools.partial(_attn_prompt_kernel, buckets),
        grid=(batch,),
        in_specs=in_specs,
        out_specs=out_specs,
        out_shape=out_shape,
        scratch_shapes=[pltpu.VMEM((2 * N_GROUPS, GROUP_HEADS * BLK, 2 * BLK), F32)],
        compiler_params=pltpu.CompilerParams(dimension_semantics=("arbitrary",), vmem_limit_bytes=VMEM_LIMIT),
        name="attn_prompt",
    )(tab, bmap, *qkvs)


def _attn_sample_init(g, t_len, buckets, in_refs, out_refs, scratch_refs):
    tab_ref, bmap_ref = in_refs[:2]
    bm_ref, tr_ref, _ = scratch_refs
    rh =lax.broadcasted_iota(jnp.int32, bm_ref.shape, 0) // t_len

    def value(bkt):
        v = jnp.full(bm_ref.shape, tab_ref[bkt, g * GROUP_HEADS], F32)
        for h in range(1, GROUP_HEADS):
            v = jnp.where(rh == h, tab_ref[bkt, g * GROUP_HEADS + h], v)
        return v

    bm_ref[...] = _bias_mask(bmap_ref[...], buckets, value)
    tr_ref[...] = jnp.zeros_like(tr_ref)


def _attn_sample_step(t_len, in_refs, out_refs, scratch_refs):
    _, _, q_ref, kvnew_ref, st_ref = in_refs
    win_ref, ol_ref = out_refs
    bm_ref, tr_ref, fb_ref = scratch_refs
    lg = st_ref.shape[2]
    nrow = GROUP_HEADS * t_len
    row_head = lax.broadcasted_iota(jnp.int32, (nrow, GW), 0) // t_len
    lane_head = lax.broadcasted_iota(jnp.int32, (nrow, GW), 1) >> 6
    ext = lg + BLK
    sel = row_head == lane_head
    for i in range(st_ref.shape[0]):
        tr_ref[i, 0:t_len, :] = kvnew_ref[i]
        for c in range(2 * GW // BLK):
            rows = slice(c * BLK, (c + 1) * BLK)
            full = jnp.concatenate([st_ref[i, rows, :], tr_ref[i, :, rows].T], axis=1)
            win_ref[i, rows, :] = pltpu.roll(full, ext - t_len, axis=1)[:, :lg]
            fb_ref[i, rows, :] = full.astype(BF16)

        qs = jnp.where(sel, q_ref[i], 0.0).astype(BF16)
        s = jnp.dot(qs, fb_ref[i, 0:GW, :], preferred_element_type=F32) + bm_ref[...]
        m = jnp.max(s, axis=-1, keepdims=True)
        p = jnp.exp(s - m)
        l = jnp.sum(p, axis=-1, keepdims=True)
        pv = lax.dot_general(p.astype(BF16), fb_ref[i, GW:2 * GW, :], (((1,), (1,)), ((), ())),
                             preferred_element_type=F32) * (1.0 / l)
        lse = jnp.broadcast_to(m + jnp.log(l), (nrow, GW))
        o = jnp.where(sel, pv, 0.0)
        ls = jnp.where(sel, lse, 0.0)
        osum, lsum = o, ls
        for h in range(1, GROUP_HEADS):
            osum = osum + pltpu.roll(o, h * t_len, axis=0)
            lsum = lsum + pltpu.roll(ls, h * t_len, axis=0)
        ol_ref[i, :, 0:GW] = osum[0:t_len]
        ol_ref[i, :, GW:2 * GW] = lsum[0:t_len]


def _sample_bucket_map(window, dil, lg, t_len):
    band = window // dil
    ext = lg + BLK
    bmap = np.full((t_len, ext), -1, np.int64)
    t = np.arange(t_len)[:, None]
    j = np.arange(lg + t_len)[None, :]
    dist = lg + t - j
    valid = (dist >= 0) & (dist % dil == 0) & (dist <= band * dil)
    bmap[:, :lg + t_len] = np.where(valid, _t5_bucket(np.maximum(dist, 0)), -1)
    return np.tile(bmap, (GROUP_HEADS, 1)).astype(np.int32)


def _attn_sample_side(g, tab, q16, kvnew, st, n_steps, step_of):
    window, dil = ATT_GROUPS[g]
    nb, _, lg = st.shape
    assert lg == window and nb % n_steps == 0
    per = nb // n_steps
    t_len = kvnew.shape[1]
    nrow = GROUP_HEADS * t_len
    bmap_np = _sample_bucket_map(window, dil, lg, t_len)
    buckets = tuple(int(b) for b in np.unique(bmap_np) if b >= 0)
    bmap = jnp.asarray(bmap_np)

    def blk(*idx):
        return (step_of(*idx), 0, 0)

    return _Side(
        name="attn_sample_g%d" % g,
        init=functools.partial(_attn_sample_init, g, t_len, buckets),
        step=functools.partial(_attn_sample_step, t_len),
        inputs=(tab, bmap, q16, kvnew, st),
        in_specs=(pl.BlockSpec(memory_space=pltpu.SMEM), _whole(bmap.shape),
                  pl.BlockSpec((per, nrow, GW), blk),
                  pl.BlockSpec((per, t_len, 2 * GW), blk),
                  pl.BlockSpec((per, 2 * GW, lg), blk)),
        out_shape=(jax.ShapeDtypeStruct((nb, 2 * GW, lg), F32),
                   jax.ShapeDtypeStruct((nb, t_len, 2 * GW), F32)),
        out_specs=(pl.BlockSpec((per, 2 * GW, lg), blk),
                   pl.BlockSpec((per, t_len, 2 * GW), blk)),
        scratch_shapes=(pltpu.VMEM((nrow, lg + BLK), F32), pltpu.VMEM((per, BLK, 2 * GW), F32),
                        pltpu.VMEM((per, 2 * GW, lg + BLK), BF16)))


def _mix_ffn_main(alpha, in_refs, out_refs, scratch_refs):
    h_ref = in_refs[0]
    ol_refs = in_refs[1:1 + N_GROUPS]
    yb_ref, sg_ref, woa_ref, wob_ref, wout_ref, g2_ref, b2_ref = in_refs[1 + N_GROUPS:8 + N_GROUPS]
    ffn_refs = in_refs[8 + N_GROUPS:]
    (o_ref,) = out_refs
    u_ref, h2_ref, h2b_ref, acc_ref = scratch_refs
    tm = h_ref.shape[0]
    ncol = 2 * GW // BLK
    ols = []
    for gi, ol_ref in enumerate(ol_refs):
        dil = ol_ref.shape[1]
        if dil == 1:
            ols.append(ol_ref[0, 0])
            continue
        for r in range(dil):
            for j in range(ncol):
                u_ref[gi, j, pl.ds(r, tm // dil, stride=dil), :] = ol_ref[0, r, :, j * BLK:(j + 1) * BLK]
        ols.append(jnp.concatenate([u_ref[gi, j] for j in range(ncol)], axis=1))
    att = _combine_groups([ol[:, :GW] for ol in ols], [ol[:, GW:] for ol in ols]).astype(BF16)
    a = jnp.dot(att, woa_ref[...], preferred_element_type=F32)
    bb = jnp.dot(yb_ref[...], wob_ref[...], preferred_element_type=F32)
    gated = sg_ref[:, :D_MODEL].astype(F32) * a + sg_ref[:, D_MODEL:].astype(F32) * bb
    mix = jnp.dot(gated.astype(BF16), wout_ref[...], preferred_element_type=F32)
    h2 = _ln(alpha * h_ref[...] + mix, g2_ref[...], b2_ref[...])
    h2_ref[...] = h2
    h2b_ref[...] = h2.astype(BF16)
    _swiglu_ln(alpha, h2_ref, h2b_ref, ffn_refs, acc_ref, o_ref)


def _mix_ffn(h2d, ols, yb, sg, woa, wob, wout, g2, b2, ffn_w, alpha, seq):
    n = h2d.shape[0]
    assert n % TM == 0 and seq % TM == 0 and len(ffn_w) == N_FFN_W
    nt = seq // TM

    def rows(a):
        return pl.BlockSpec((TM, a.shape[1]), lambda i: (i, 0))

    ins = [h2d, *ols, yb, sg, woa, wob, wout, g2, b2, *ffn_w]
    in_specs = [rows(h2d)]
    for a in ols:
        dil = a.shape[1]
        in_specs.append(pl.BlockSpec((1, dil, TM // dil, 2 * GW), lambda i: (i // nt, 0, i % nt, 0)))
    in_specs += ([rows(yb), rows(sg)] + [_whole(a.shape) for a in (woa, wob, wout, g2, b2)] + _ffn_specs())
    (out,), _ = _call(
        "mix_ffn", functools.partial(_mix_ffn_main, alpha), None, (n // TM,), ins, in_specs,
        [jax.ShapeDtypeStruct((n, D_MODEL), F32)], [rows(h2d)],
        [pltpu.VMEM((N_GROUPS, 2 * GW // BLK, TM, BLK), F32), pltpu.VMEM((TM, D_MODEL), F32),
         pltpu.VMEM((TM, D_MODEL), BF16), pltpu.VMEM((TM, D_MODEL), F32)])
    return out


def _prep_ffn(w1, w3, w2, ln_g, ln_b):
    return (w1.astype(BF16), w3.astype(BF16), w2.reshape(NC, FC, D_MODEL).astype(BF16),
            ln_g[None, :], ln_b[None, :])


def _prep_w_in(w_in, b_in):
    def cols(a, lo, n):
        return a[..., lo:lo + n]
    wq, bq = [], []
    for g in range(N_GROUPS):
        parts = [(i * ATT_W + g * GW, GW) for i in range(3)]
        wq.append(jnp.concatenate([cols(w_in, lo, n) for lo, n in parts], axis=-1))
        bq.append(jnp.concatenate([cols(b_in, lo, n) for lo, n in parts], axis=-1))
    o_uv = 3 * ATT_W
    return (jnp.stack(wq).astype(BF16), jnp.stack(bq)[:, None, :],
            cols(w_in, o_uv, 2 * SGU_W).astype(BF16), cols(b_in, o_uv, 2 * SGU_W)[None, :],
            cols(w_in, o_uv + 2 * SGU_W, 2 * D_MODEL).astype(BF16),
            cols(b_in, o_uv + 2 * SGU_W, 2 * D_MODEL)[None, :])


def _to_window(a):
    return a.reshape(a.shape[0], a.shape[1], 2, GROUP_HEADS, HEAD_DIM)


def kernel(x_prompt, x_sample, state_win0, state_win1, state_win2, rel_bias, ln1_g, ln1_b, f1_w1, f1_w3, f1_w2, w_in, b_in, sgu_ln_g, sgu_ln_b, sgu_ws, sgu_b, w_oa, w_ob, w_out, ln2_g, ln2_b, f2_w1, f2_w3, f2_w2, ln3_g, ln3_b):
    depth = ln1_g.shape[0]
    alpha = (2 * depth) ** 0.25
    batch, seq, _ = x_prompt.shape
    dec_batch, dec_seq, _ = x_sample.shape
    states = (state_win0, state_win1, state_win2)

    yp = x_prompt.reshape(batch * seq, D_MODEL)
    ys = x_sample.reshape(dec_batch * dec_seq, D_MODEL)
    outs = [[] for _ in range(7)]
    for l in range(depth):
        f1 = _prep_ffn(f1_w1[l], f1_w3[l], f1_w2[l], ln1_g[l], ln1_b[l])
        f2 = _prep_ffn(f2_w1[l], f2_w3[l], f2_w2[l], ln3_g[l], ln3_b[l])
        wqkv, bqkv, wuv, buv, wg, bg = _prep_w_in(w_in[l], b_in[l])
        sln = (sgu_ln_g[l][None, :], sgu_ln_b[l][None, :])
        woa, wob, wout = w_oa[l].astype(BF16), w_ob[l].astype(BF16), w_out[l].astype(BF16)
        ln2 = (ln2_g[l][None, :], ln2_b[l][None, :])

        hs, _ = _ffn_ln(ys, f1, alpha)
        ws4 = sgu_ws[l][:, :dec_seq, :dec_seq].reshape(SGU_GROUPS, dec_seq * dec_seq)
        sb4 = sgu_b[l][:, :dec_seq]
        qs, kvs, ybs, sgs, vn = _proj_sample(hs, dec_seq, ws4, sb4, wqkv, bqkv, wuv, buv, wg, bg, *sln)
        nt = seq // TM
        n_steps = batch * nt
        sides = []
        for g in range(N_GROUPS):
            st = states[g][l]
            lg = st.shape[1]
            st_cm = jnp.transpose(st, (0, 2, 3, 4, 1)).reshape(dec_batch, 2 * GW, lg)
            q16 = jnp.tile(qs[:, g * GW:(g + 1) * GW].reshape(dec_batch, 1, dec_seq, GW),
                           (1, GROUP_HEADS, 1, 1)).reshape(dec_batch, GROUP_HEADS * dec_seq, GW)
            kvnew = kvs[:, g * 2 * GW:(g + 1) * 2 * GW].reshape(dec_batch, dec_seq, 2 * GW)
            step_of = (lambda i: i) if g == N_GROUPS - 1 else (lambda b, t: b * nt + t)
            sides.append(_attn_sample_side(g, rel_bias, q16, kvnew, st_cm, n_steps, step_of))

        hp, side_a = _ffn_ln(yp, f1, alpha, sides=sides[2:])
        res, side_b = _proj_prompt(hp, batch, seq, wqkv, bqkv, wuv, buv, wg, bg, *sln,
                                   sgu_ws[l], jnp.transpose(sgu_b[l]), sides=sides[:2])
        qkvs, wins, ybp, sgp = res[0:3], res[3:6], res[6], res[7]
        ols = _attn_prompt(rel_bias, qkvs, batch)
        yp = _mix_ffn(hp, ols, ybp, sgp, woa, wob, wout, *ln2, f2, alpha, seq)
        for g in range(N_GROUPS):
            outs[g].append(_to_window(wins[g]))

        ols_s = []
        for g, (win_cm, ol) in enumerate(side_b + side_a):
            lg = win_cm.shape[2]
            win = jnp.transpose(win_cm.reshape(dec_batch, 2, GROUP_HEADS, HEAD_DIM, lg), (0, 4, 1, 2, 3))
            outs[3 + g].append(win)
            ols_s.append(ol.reshape(1, 1, dec_batch * dec_seq, 2 * GW))
        ys = _mix_ffn(hs, ols_s, ybs, sgs, woa, wob, wout, *ln2, f2, alpha, dec_batch * dec_seq)
        outs[6].append(vn.reshape(dec_batch, dec_seq, SGU_W))

    return (yp.reshape(batch, seq, D_MODEL), ys.reshape(dec_batch, dec_seq, D_MODEL),
            *[jnp.stack(o) for o in outs])
```

```python
import functools
import math
from typing import Callable, NamedTuple

import jax
import jax.numpy as jnp
import numpy as np
from jax import lax
from jax.experimental import pallas as pl
from jax.experimental.pallas import tpu as pltpu

F32 = jnp.float32
BF16 = jnp.bfloat16

D_MODEL = 1024
HEAD_DIM = 64
GROUP_HEADS = 4
ATT_GROUPS = ((128, 1), (512, 4), (2048, 16))
N_GROUPS = len(ATT_GROUPS)
N_ATT_HEADS = N_GROUPS * GROUP_HEADS
ATT_W = N_ATT_HEADS * HEAD_DIM
GW = GROUP_HEADS * HEAD_DIM
BLK = 128
N_BUCKETS = 32
MAX_DISTANCE = 2048
SGU_CHUNK = 128
SGU_GROUPS = 4
SGU_GROUP_W = 128
SGU_W = SGU_GROUPS * SGU_GROUP_W
D_FF = 2816
LN_EPS = 1e-5
NEG = -1e30
SCALE = HEAD_DIM ** -0.5

TM = 512
FC = 256
NC = D_FF // FC
ATTN_UNROLL = 16
ATTN_UNROLL_SINGLE = 4
VMEM_LIMIT = 60 * 1024 * 1024


def _t5_bucket(dist):
    dist = np.asarray(dist, np.int64)
    max_exact = N_BUCKETS // 2
    large = max_exact + (np.log(np.maximum(dist, max_exact) / max_exact)
                         / np.log(MAX_DISTANCE / max_exact) * (N_BUCKETS - max_exact)).astype(np.int64)
    large = np.minimum(large, N_BUCKETS - 1)
    return np.where(dist < max_exact, dist, large).astype(np.int32)


def _ln(t, g, b):
    mu = jnp.mean(t, axis=-1, keepdims=True)
    d = t - mu
    var = jnp.mean(d * d, axis=-1, keepdims=True)
    return d * lax.rsqrt(var + LN_EPS) * g + b


def _whole(shape):
    nd = len(shape)
    return pl.BlockSpec(shape, lambda *_: (0,) * nd, pipeline_mode=pl.Buffered(1))


class _Side(NamedTuple):
    name: str
    init: Callable
    step: Callable
    inputs: tuple
    in_specs: tuple
    out_shape: tuple
    out_specs: tuple
    scratch_shapes: tuple


def _take(refs, pos, count):
    return refs[pos:pos + count], pos + count


def _call(name, main, main_init, grid, inputs, in_specs, out_shape, out_specs, scratch, sides=()):
    n_in, n_out, n_scr = len(inputs), len(out_shape), len(scratch)

    def body(*refs):
        refs = list(refs)
        m_in, pos = _take(refs, 0, n_in)
        s_in = []
        for s in sides:
            r, pos = _take(refs, pos, len(s.inputs))
            s_in.append(r)
        m_out, pos = _take(refs, pos, n_out)
        s_out = []
        for s in sides:
            r, pos = _take(refs, pos, len(s.out_shape))
            s_out.append(r)
        m_scr, pos = _take(refs, pos, n_scr)
        s_scr = []
        for s in sides:
            r, pos = _take(refs, pos, len(s.scratch_shapes))
            s_scr.append(r)

        if main_init is not None or sides:
            first = pl.program_id(0) == 0
            for axis in range(1, len(grid)):
                first = jnp.logical_and(first, pl.program_id(axis) == 0)

            @pl.when(first)
            def _():
                if main_init is not None:
                    main_init(m_in, m_out, m_scr)
                for s, i, o, c in zip(sides, s_in, s_out, s_scr):
                    s.init(i, o, c)

        main(m_in, m_out, m_scr)
        for s, i, o, c in zip(sides, s_in, s_out, s_scr):
            s.step(i, o, c)

    all_in = list(inputs) + [a for s in sides for a in s.inputs]
    all_in_specs = list(in_specs) + [a for s in sides for a in s.in_specs]
    all_out_shape = list(out_shape) + [a for s in sides for a in s.out_shape]
    all_out_specs = list(out_specs) + [a for s in sides for a in s.out_specs]
    all_scratch = list(scratch) + [a for s in sides for a in s.scratch_shapes]
    res = pl.pallas_call(
        body,
        grid=grid,
        in_specs=all_in_specs,
        out_specs=all_out_specs,
        out_shape=all_out_shape,
        scratch_shapes=all_scratch,
        compiler_params=pltpu.CompilerParams(dimension_semantics=("arbitrary",) * len(grid),
                                             vmem_limit_bytes=VMEM_LIMIT),
        name=name + "".join("_" + s.name for s in sides),
    )(*all_in)
    res = list(res)
    main_res, pos = _take(res, 0, n_out)
    side_res = []
    for s in sides:
        r, pos = _take(res, pos, len(s.out_shape))
        side_res.append(r)
    return main_res, side_res


def _swiglu_ln(alpha, x_ref, xb_ref, w_refs, hid_ref, o_ref):
    w1_ref, w3_ref, w2_ref, g_ref, b_ref = w_refs
    for c in range(NC):
        cols = slice(c * FC, (c + 1) * FC)
        a = jnp.dot(xb_ref[...], w1_ref[:, cols], preferred_element_type=F32)
        b = jnp.dot(xb_ref[...], w3_ref[:, cols], preferred_element_type=F32)
        hid_ref[:, cols] = ((a * jax.nn.sigmoid(a)) * b).astype(BF16)
    ffn = jnp.dot(hid_ref[...], w2_ref[...], preferred_element_type=F32)
    t = alpha * x_ref[...] + 0.5 * ffn
    o_ref[...] = _ln(t, g_ref[...], b_ref[...])


N_FFN_W = 5


def _ffn_specs():
    return [_whole((D_MODEL, D_FF)), _whole((D_MODEL, D_FF)), _whole((D_FF, D_MODEL)),
            _whole((1, D_MODEL)), _whole((1, D_MODEL))]


def _ffn_ln_main(alpha, in_refs, out_refs, scratch_refs):
    x_ref = in_refs[0]
    xb_ref, hid_ref = scratch_refs
    xb_ref[...] = x_ref[...].astype(BF16)
    _swiglu_ln(alpha, x_ref, xb_ref, in_refs[1:1 + N_FFN_W], hid_ref, out_refs[0])


def _ffn_ln(x2d, ffn_w, alpha, sides=()):
    n = x2d.shape[0]
    assert n % TM == 0 and len(ffn_w) == N_FFN_W
    (out,), side_res = _call(
        "ffn_ln", functools.partial(_ffn_ln_main, alpha), None, (n // TM,),
        [x2d, *ffn_w],
        [pl.BlockSpec((TM, D_MODEL), lambda i: (i, 0))] + _ffn_specs(),
        [jax.ShapeDtypeStruct((n, D_MODEL), F32)],
        [pl.BlockSpec((TM, D_MODEL), lambda i: (i, 0))],
        [pltpu.VMEM((TM, D_MODEL), BF16), pltpu.VMEM((TM, D_FF), BF16)],
        sides)
    return out, side_res


def _gelu_exact(z):
    return 0.5 * z * (1.0 + lax.erf(z * (1.0 / math.sqrt(2.0))))


def _uv_branch(hb, wuv_ref, buv_ref, sg_ref, sb_ref):
    zuv = jnp.dot(hb, wuv_ref[...], preferred_element_type=F32) + buv_ref[...]
    zz = _gelu_exact(zuv)
    u = zz[:, :SGU_W]
    vn = _ln(zz[:, SGU_W:], sg_ref[...], sb_ref[...])
    return u, vn


def _gates(hb, wg_ref, bg_ref, sg_out_ref):
    zg = jnp.dot(hb, wg_ref[...], preferred_element_type=F32) + bg_ref[...]
    sg_out_ref[...] = jax.nn.sigmoid(zg).astype(BF16)


def _proj_prompt_main(in_refs, out_refs, scratch_refs):
    (h_ref, wqkv_ref, bqkv_ref, wuv_ref, buv_ref, wg_ref, bg_ref,
     sln_g_ref, sln_b_ref, ws_ref, sbt_ref) = in_refs
    qkv_refs, win_refs = out_refs[0:3], out_refs[3:6]
    yb_ref, sg_ref = out_refs[6:8]
    zs_ref, hb_ref = scratch_refs
    hb_ref[...] = h_ref[...].astype(BF16)
    u, vn = _uv_branch(hb_ref[...], wuv_ref, buv_ref, sln_g_ref, sln_b_ref)
    _gates(hb_ref[...], wg_ref, bg_ref, sg_ref)

    vnb = vn.astype(BF16)
    row = lax.broadcasted_iota(jnp.int32, (SGU_CHUNK, SGU_CHUNK), 0)
    col = lax.broadcasted_iota(jnp.int32, (SGU_CHUNK, SGU_CHUNK), 1)
    for g in range(SGU_GROUPS):
        wm = jnp.where(row >= col, ws_ref[g], 0.0).astype(BF16)
        bias = jnp.broadcast_to(sbt_ref[:, g:g + 1], (SGU_CHUNK, SGU_GROUP_W))
        cs = slice(g * SGU_GROUP_W, (g + 1) * SGU_GROUP_W)
        for c in range(TM // SGU_CHUNK):
            rs = slice(c * SGU_CHUNK, (c + 1) * SGU_CHUNK)
            mixed = jnp.dot(wm, vnb[rs, cs], preferred_element_type=F32) + bias
            yb_ref[rs, cs] = (u[rs, cs] * mixed).astype(BF16)

    for g, (window, dil) in enumerate(ATT_GROUPS):
        zg = jnp.dot(hb_ref[...], wqkv_ref[g], preferred_element_type=F32) + bqkv_ref[g]
        keep = min(window, TM)
        win_refs[g][0] = zg[TM - keep:, GW:]
        zq = jnp.concatenate([zg[:, :GW] * SCALE, zg[:, GW:]], axis=1)
        if dil == 1:
            qkv_refs[g][0, 0] = zq.astype(BF16)
            continue
        ncol = 3 * GW // BLK
        for j in range(ncol):
            zs_ref[j] = zq[:, j * BLK:(j + 1) * BLK]
        rows = TM // dil
        for r in range(dil):
            sub = jnp.concatenate([zs_ref[j, pl.ds(r, rows, stride=dil), :] for j in range(ncol)], axis=1)
            qkv_refs[g][0, r] = sub.astype(BF16)


def _proj_prompt(h2d, batch, seq, wqkv, bqkv, wuv, buv, wg, bg, sln_g, sln_b, ws, sbt, sides=()):
    nt = seq // TM
    assert seq % TM == 0 and TM == ATT_GROUPS[1][0] and TM >= ATT_GROUPS[0][0]
    ins = [h2d, wqkv, bqkv, wuv, buv, wg, bg, sln_g, sln_b, ws, sbt]
    in_specs = [pl.BlockSpec((TM, D_MODEL), lambda b, t: (b * nt + t, 0))] + [_whole(a.shape) for a in ins[1:]]
    out_shape, out_specs = [], []
    for window, dil in ATT_GROUPS:
        out_shape.append(jax.ShapeDtypeStruct((batch, dil, seq // dil, 3 * GW), BF16))
        out_specs.append(pl.BlockSpec((1, dil, TM // dil, 3 * GW), lambda b, t: (b, 0, t, 0)))
    for window, dil in ATT_GROUPS:
        keep = min(window, seq)
        out_shape.append(jax.ShapeDtypeStruct((batch, keep, 2 * GW), F32))
        if keep > TM:
            assert keep == seq
            out_specs.append(pl.BlockSpec((1, TM, 2 * GW), lambda b, t: (b, t, 0)))
        else:
            out_specs.append(pl.BlockSpec((1, keep, 2 * GW), lambda b, t: (b, 0, 0)))
    out_shape += [jax.ShapeDtypeStruct((batch * seq, SGU_W), BF16),
                  jax.ShapeDtypeStruct((batch * seq, 2 * D_MODEL), BF16)]
    out_specs += [pl.BlockSpec((TM, SGU_W), lambda b, t: (b * nt + t, 0)),
                  pl.BlockSpec((TM, 2 * D_MODEL), lambda b, t: (b * nt + t, 0))]
    return _call("proj_prompt", _proj_prompt_main, None, (batch, nt), ins, in_specs, out_shape, out_specs,
                 [pltpu.VMEM((3 * GW // BLK, TM, BLK), F32), pltpu.VMEM((TM, D_MODEL), BF16)], sides)


def _proj_sample_kernel(t_len, ws4_ref, sb4_ref, h_ref, wqkv_ref, bqkv_ref, wuv_ref, buv_ref,
                        wg_ref, bg_ref, sln_g_ref, sln_b_ref,
                        q_ref, kv_ref, yb_ref, sg_ref, vn_ref):
    n = h_ref.shape[0]
    hb = h_ref[...].astype(BF16)
    for g in range(N_GROUPS):
        zg = jnp.dot(hb, wqkv_ref[g], preferred_element_type=F32) + bqkv_ref[g]
        q_ref[:, g * GW:(g + 1) * GW] = zg[:, :GW] * SCALE
        kv_ref[:, g * 2 * GW:(g + 1) * 2 * GW] = zg[:, GW:]

    u, vn = _uv_branch(hb, wuv_ref, buv_ref, sln_g_ref, sln_b_ref)
    vn_ref[...] = vn
    p = lax.broadcasted_iota(jnp.int32, (n, SGU_GROUP_W), 0) & (t_len - 1)
    for g in range(SGU_GROUPS):
        cs = slice(g * SGU_GROUP_W, (g + 1) * SGU_GROUP_W)
        vg = vn[:, cs]
        acc = jnp.zeros((n, SGU_GROUP_W), F32)
        for pp in range(t_len):
            acc = jnp.where(p == pp, sb4_ref[g, pp], acc)
        for d in range(t_len):
            coef = jnp.zeros((n, SGU_GROUP_W), F32)
            for pp in range(d, t_len):
                coef = jnp.where(p == pp, ws4_ref[g, pp * t_len + (pp - d)], coef)
            shifted = vg if d == 0 else pltpu.roll(vg, d, axis=0)
            acc = acc + coef * shifted
        yb_ref[:, cs] = (u[:, cs] * acc).astype(BF16)

    _gates(hb, wg_ref, bg_ref, sg_ref)


def _proj_sample(h2d, t_len, ws4, sb4, wqkv, bqkv, wuv, buv, wg, bg, sln_g, sln_b):
    n = h2d.shape[0]
    assert t_len & (t_len - 1) == 0
    smem = pl.BlockSpec(memory_space=pltpu.SMEM)
    ins = (h2d, wqkv, bqkv, wuv, buv, wg, bg, sln_g, sln_b)
    out_shape = [jax.ShapeDtypeStruct((n, N_GROUPS * GW), F32),
                 jax.ShapeDtypeStruct((n, N_GROUPS * 2 * GW), F32),
                 jax.ShapeDtypeStruct((n, SGU_W), BF16),
                 jax.ShapeDtypeStruct((n, 2 * D_MODEL), BF16),
                 jax.ShapeDtypeStruct((n, SGU_W), F32)]
    return pl.pallas_call(
        functools.partial(_proj_sample_kernel, t_len),
        grid=(1,),
        in_specs=[smem, smem] + [_whole(a.shape) for a in ins],
        out_specs=[_whole(s.shape) for s in out_shape],
        out_shape=out_shape,
        compiler_params=pltpu.CompilerParams(dimension_semantics=("arbitrary",), vmem_limit_bytes=VMEM_LIMIT),
        name="proj_sample",
    )(ws4, sb4, *ins)


def _bias_mask(bmap, buckets, value_of_bucket):
    acc = jnp.full(bmap.shape, NEG, F32)
    for bkt in buckets:
        acc = jnp.where(bmap == bkt, value_of_bucket(bkt), acc)
    return acc


def _prompt_bucket_map(window, dil):
    band = window // dil
    rel = np.arange(BLK)[:, None] - np.arange(2 * BLK)[None, :] + BLK
    valid = (rel >= 0) & (rel <= band)
    return np.where(valid, _t5_bucket(np.clip(rel, 0, band) * dil), -1).astype(np.int32)


def _combine_groups(outs, lses):
    mx = jnp.maximum(jnp.maximum(lses[0], lses[1]), lses[2])
    es = [jnp.exp(ls - mx) for ls in lses]
    den = es[0] + es[1] + es[2]
    num = es[0] * outs[0] + es[1] * outs[1] + es[2] * outs[2]
    return num * (1.0 / den)


def _attn_prompt_kernel(buckets, tab_ref, bmap_ref, qkv0_ref, qkv1_ref, qkv2_ref,
                        ol0_ref, ol1_ref, ol2_ref, bm_ref):
    @pl.when(pl.program_id(0) == 0)
    def _():
        for g in range(N_GROUPS):
            bmap = bmap_ref[g]
            for h in range(GROUP_HEADS):
                rs = slice(h * BLK, (h + 1) * BLK)
                bm = _bias_mask(bmap, buckets[g], lambda bkt: tab_ref[bkt, g * GROUP_HEADS + h])
                bm_ref[2 * g, rs, :] = bm
                bm_ref[2 * g + 1, rs, :BLK] = bm[:, BLK:]
                bm_ref[2 * g + 1, rs, BLK:] = jnp.full((BLK, BLK), NEG, F32)

    lane_head = lax.broadcasted_iota(jnp.int32, (BLK, GW), 1) >> 6
    lane_lo = lax.broadcasted_iota(jnp.int32, (BLK, BLK), 1) < HEAD_DIM

    def unit(g, qkv_ref, ol_ref, r, n):
        single = qkv_ref.shape[2] == BLK
        static_n = isinstance(n, int)
        q0 = n * BLK if static_n else pl.multiple_of(n * BLK, BLK)
        q = qkv_ref[0, r, pl.ds(q0, BLK), 0:GW]
        qs = jnp.concatenate([jnp.where(lane_head == h, q, jnp.zeros_like(q))
                              for h in range(GROUP_HEADS)], axis=0)
        if single:
            k = qkv_ref[0, r, :, GW:2 * GW]
            v = qkv_ref[0, r, :, 2 * GW:3 * GW]
            bm = bm_ref[2 * g, :, BLK:]
        else:
            if static_n:
                k0 = max(q0 - BLK, 0)
                bm = bm_ref[2 * g + int(n == 0)]
            else:
                k0 = pl.multiple_of(jnp.maximum(q0 - BLK, 0), BLK)
                bm = bm_ref[2 * g + (n == 0).astype(jnp.int32)]
            k = qkv_ref[0, r, pl.ds(k0, 2 * BLK), GW:2 * GW]
            v = qkv_ref[0, r, pl.ds(k0, 2 * BLK), 2 * GW:3 * GW]
        s = lax.dot_general(qs, k, (((1,), (1,)), ((), ())), preferred_element_type=F32) + bm
        m = jnp.max(s, axis=-1, keepdims=True)
        p = jnp.exp(s - m)
        l = jnp.sum(p, axis=-1, keepdims=True)
        pb = p.astype(BF16)
        linv = 1.0 / l
        lse = m + jnp.log(l)
        rows = pl.ds(q0, BLK)
        for half in range(2):
            pv = jnp.dot(pb[2 * half * BLK:(2 * half + 2) * BLK], v[:, half * BLK:(half + 1) * BLK],
                         preferred_element_type=F32)
            ra = slice(2 * half * BLK, (2 * half + 1) * BLK)
            rb = slice((2 * half + 1) * BLK, (2 * half + 2) * BLK)
            ol_ref[0, r, rows, half * BLK:(half + 1) * BLK] = jnp.where(
                lane_lo, pv[:BLK] * linv[ra], pv[BLK:] * linv[rb])
            ol_ref[0, r, rows, GW + half * BLK:GW + (half + 1) * BLK] = jnp.where(lane_lo, lse[ra], lse[rb])

    for g, (qkv_ref, ol_ref) in enumerate(((qkv0_ref, ol0_ref), (qkv1_ref, ol1_ref), (qkv2_ref, ol2_ref))):
        dil, sub_len = qkv_ref.shape[1], qkv_ref.shape[2]
        nb = sub_len // BLK
        n_units = dil * nb
        unroll = ATTN_UNROLL_SINGLE if nb == 1 else ATTN_UNROLL
        assert n_units % unroll == 0 and (nb == 1 or unroll % nb == 0 or dil == 1)

        def trip(t, carry, g=g, qkv_ref=qkv_ref, ol_ref=ol_ref, nb=nb, dil=dil, unroll=unroll):
            for j in range(unroll):
                if nb == 1:
                    r, n = t * unroll + j, 0
                elif dil == 1:
                    r, n = 0, t * unroll + j
                else:
                    r, n = t * (unroll // nb) + j // nb, j % nb
                unit(g, qkv_ref, ol_ref, r, n)
            return carry

        lax.fori_loop(0, n_units // unroll, trip, 0)


def _attn_prompt(tab, qkvs, batch):
    bmaps = [_prompt_bucket_map(w, d) for w, d in ATT_GROUPS]
    buckets = tuple(tuple(int(b) for b in np.unique(m) if b >= 0) for m in bmaps)
    bmap = jnp.asarray(np.stack(bmaps))
    in_specs = [pl.BlockSpec(memory_space=pltpu.SMEM), _whole(bmap.shape)]
    out_shape, out_specs = [], []
    for a in qkvs:
        _, dil, sub_len, _ = a.shape
        in_specs.append(pl.BlockSpec((1, dil, sub_len, 3 * GW), lambda b: (b, 0, 0, 0)))
        out_shape.append(jax.ShapeDtypeStruct((batch, dil, sub_len, 2 * GW), F32))
        out_specs.append(pl.BlockSpec((1, dil, sub_len, 2 * GW), lambda b: (b, 0, 0, 0)))
    return pl.pallas_call(
        functools.partial(_attn_prompt_kernel, buckets),
        grid=(batch,),
        in_specs=in_specs,
        out_specs=out_specs,
        out_shape=out_shape,
        scratch_shapes=[pltpu.VMEM((2 * N_GROUPS, GROUP_HEADS * BLK, 2 * BLK), F32)],
        compiler_params=pltpu.CompilerParams(dimension_semantics=("arbitrary",), vmem_limit_bytes=VMEM_LIMIT),
        name="attn_prompt",
    )(tab, bmap, *qkvs)


def _attn_sample_init(g, t_len, buckets, in_refs, out_refs, scratch_refs):
    tab_ref, bmap_ref = in_refs[:2]
    bm_ref, tr_ref, _ = scratch_refs
    rh =lax.broadcasted_iota(jnp.int32, bm_ref.shape, 0) // t_len

    def value(bkt):
        v = jnp.full(bm_ref.shape, tab_ref[bkt, g * GROUP_HEADS], F32)
        for h in range(1, GROUP_HEADS):
            v = jnp.where(rh == h, tab_ref[bkt, g * GROUP_HEADS + h], v)
        return v

    bm_ref[...] = _bias_mask(bmap_ref[...], buckets, value)
    tr_ref[...] = jnp.zeros_like(tr_ref)


def _attn_sample_step(t_len, in_refs, out_refs, scratch_refs):
    _, _, q_ref, kvnew_ref, st_ref = in_refs
    win_ref, ol_ref = out_refs
    bm_ref, tr_ref, fb_ref = scratch_refs
    lg = st_ref.shape[2]
    nrow = GROUP_HEADS * t_len
    row_head = lax.broadcasted_iota(jnp.int32, (nrow, GW), 0) // t_len
    lane_head = lax.broadcasted_iota(jnp.int32, (nrow, GW), 1) >> 6
    ext = lg + BLK
    sel = row_head == lane_head
    for i in range(st_ref.shape[0]):
        tr_ref[i, 0:t_len, :] = kvnew_ref[i]
        for c in range(2 * GW // BLK):
            rows = slice(c * BLK, (c + 1) * BLK)
            full = jnp.concatenate([st_ref[i, rows, :], tr_ref[i, :, rows].T], axis=1)
            win_ref[i, rows, :] = pltpu.roll(full, ext - t_len, axis=1)[:, :lg]
            fb_ref[i, rows, :] = full.astype(BF16)

        qs = jnp.where(sel, q_ref[i], 0.0).astype(BF16)
        s = jnp.dot(qs, fb_ref[i, 0:GW, :], preferred_element_type=F32) + bm_ref[...]
        m = jnp.max(s, axis=-1, keepdims=True)
        p = jnp.exp(s - m)
        l = jnp.sum(p, axis=-1, keepdims=True)
        pv = lax.dot_general(p.astype(BF16), fb_ref[i, GW:2 * GW, :], (((1,), (1,)), ((), ())),
                             preferred_element_type=F32) * (1.0 / l)
        lse = jnp.broadcast_to(m + jnp.log(l), (nrow, GW))
        o = jnp.where(sel, pv, 0.0)
        ls = jnp.where(sel, lse, 0.0)
        osum, lsum = o, ls
        for h in range(1, GROUP_HEADS):
            osum = osum + pltpu.roll(o, h * t_len, axis=0)
            lsum = lsum + pltpu.roll(ls, h * t_len, axis=0)
        ol_ref[i, :, 0:GW] = osum[0:t_len]
        ol_ref[i, :, GW:2 * GW] = lsum[0:t_len]


def _sample_bucket_map(window, dil, lg, t_len):
    band = window // dil
    ext = lg + BLK
    bmap = np.full((t_len, ext), -1, np.int64)
    t = np.arange(t_len)[:, None]
    j = np.arange(lg + t_len)[None, :]
    dist = lg + t - j
    valid = (dist >= 0) & (dist % dil == 0) & (dist <= band * dil)
    bmap[:, :lg + t_len] = np.where(valid, _t5_bucket(np.maximum(dist, 0)), -1)
    return np.tile(bmap, (GROUP_HEADS, 1)).astype(np.int32)


def _attn_sample_side(g, tab, q16, kvnew, st, n_steps, step_of):
    window, dil = ATT_GROUPS[g]
    nb, _, lg = st.shape
    assert lg == window and nb % n_steps == 0
    per = nb // n_steps
    t_len = kvnew.shape[1]
    nrow = GROUP_HEADS * t_len
    bmap_np = _sample_bucket_map(window, dil, lg, t_len)
    buckets = tuple(int(b) for b in np.unique(bmap_np) if b >= 0)
    bmap = jnp.asarray(bmap_np)

    def blk(*idx):
        return (step_of(*idx), 0, 0)

    return _Side(
        name="attn_sample_g%d" % g,
        init=functools.partial(_attn_sample_init, g, t_len, buckets),
        step=functools.partial(_attn_sample_step, t_len),
        inputs=(tab, bmap, q16, kvnew, st),
        in_specs=(pl.BlockSpec(memory_space=pltpu.SMEM), _whole(bmap.shape),
                  pl.BlockSpec((per, nrow, GW), blk),
                  pl.BlockSpec((per, t_len, 2 * GW), blk),
                  pl.BlockSpec((per, 2 * GW, lg), blk)),
        out_shape=(jax.ShapeDtypeStruct((nb, 2 * GW, lg), F32),
                   jax.ShapeDtypeStruct((nb, t_len, 2 * GW), F32)),
        out_specs=(pl.BlockSpec((per, 2 * GW, lg), blk),
                   pl.BlockSpec((per, t_len, 2 * GW), blk)),
        scratch_shapes=(pltpu.VMEM((nrow, lg + BLK), F32), pltpu.VMEM((per, BLK, 2 * GW), F32),
                        pltpu.VMEM((per, 2 * GW, lg + BLK), BF16)))


def _mix_ffn_main(alpha, in_refs, out_refs, scratch_refs):
    h_ref = in_refs[0]
    ol_refs = in_refs[1:1 + N_GROUPS]
    yb_ref, sg_ref, woa_ref, wob_ref, wout_ref, g2_ref, b2_ref = in_refs[1 + N_GROUPS:8 + N_GROUPS]
    ffn_refs = in_refs[8 + N_GROUPS:]
    (o_ref,) = out_refs
    u_ref, h2_ref, h2b_ref, hid_ref = scratch_refs
    tm = h_ref.shape[0]
    ncol = 2 * GW // BLK
    ols = []
    for gi, ol_ref in enumerate(ol_refs):
        dil = ol_ref.shape[1]
        if dil == 1:
            ols.append(ol_ref[0, 0])
            continue
        for r in range(dil):
            for j in range(ncol):
                u_ref[gi, j, pl.ds(r, tm // dil, stride=dil), :] = ol_ref[0, r, :, j * BLK:(j + 1) * BLK]
        ols.append(jnp.concatenate([u_ref[gi, j] for j in range(ncol)], axis=1))
    att = _combine_groups([ol[:, :GW] for ol in ols], [ol[:, GW:] for ol in ols]).astype(BF16)
    a = jnp.dot(att, woa_ref[...], preferred_element_type=F32)
    bb = jnp.dot(yb_ref[...], wob_ref[...], preferred_element_type=F32)
    gated = sg_ref[:, :D_MODEL].astype(F32) * a + sg_ref[:, D_MODEL:].astype(F32) * bb
    mix = jnp.dot(gated.astype(BF16), wout_ref[...], preferred_element_type=F32)
    h2 = _ln(alpha * h_ref[...] + mix, g2_ref[...], b2_ref[...])
    h2_ref[...] = h2
    h2b_ref[...] = h2.astype(BF16)
    _swiglu_ln(alpha, h2_ref, h2b_ref, ffn_refs, hid_ref, o_ref)


def _mix_ffn(h2d, ols, yb, sg, woa, wob, wout, g2, b2, ffn_w, alpha, seq):
    n = h2d.shape[0]
    assert n % TM == 0 and seq % TM == 0 and len(ffn_w) == N_FFN_W
    nt = seq // TM

    def rows(a):
        return pl.BlockSpec((TM, a.shape[1]), lambda i: (i, 0))

    ins = [h2d, *ols, yb, sg, woa, wob, wout, g2, b2, *ffn_w]
    in_specs = [rows(h2d)]
    for a in ols:
        dil = a.shape[1]
        in_specs.append(pl.BlockSpec((1, dil, TM // dil, 2 * GW), lambda i: (i // nt, 0, i % nt, 0)))
    in_specs += ([rows(yb), rows(sg)] + [_whole(a.shape) for a in (woa, wob, wout, g2, b2)] + _ffn_specs())
    (out,), _ = _call(
        "mix_ffn", functools.partial(_mix_ffn_main, alpha), None, (n // TM,), ins, in_specs,
        [jax.ShapeDtypeStruct((n, D_MODEL), F32)], [rows(h2d)],
        [pltpu.VMEM((N_GROUPS, 2 * GW // BLK, TM, BLK), F32), pltpu.VMEM((TM, D_MODEL), F32),
         pltpu.VMEM((TM, D_MODEL), BF16), pltpu.VMEM((TM, D_FF), BF16)])
    return out


def _prep_ffn(w1, w3, w2, ln_g, ln_b):
    return (w1.astype(BF16), w3.astype(BF16), w2.astype(BF16), ln_g[None, :], ln_b[None, :])


def _prep_w_in(w_in, b_in):
    def cols(a, lo, n):
        return a[..., lo:lo + n]
    wq, bq = [], []
    for g in range(N_GROUPS):
        parts = [(i * ATT_W + g * GW, GW) for i in range(3)]
        wq.append(jnp.concatenate([cols(w_in, lo, n) for lo, n in parts], axis=-1))
        bq.append(jnp.concatenate([cols(b_in, lo, n) for lo, n in parts], axis=-1))
    o_uv = 3 * ATT_W
    return (jnp.stack(wq).astype(BF16), jnp.stack(bq)[:, None, :],
            cols(w_in, o_uv, 2 * SGU_W).astype(BF16), cols(b_in, o_uv, 2 * SGU_W)[None, :],
            cols(w_in, o_uv + 2 * SGU_W, 2 * D_MODEL).astype(BF16),
            cols(b_in, o_uv + 2 * SGU_W, 2 * D_MODEL)[None, :])


def _to_window(a):
    return a.reshape(a.shape[0], a.shape[1], 2, GROUP_HEADS, HEAD_DIM)


def kernel(x_prompt, x_sample, state_win0, state_win1, state_win2, rel_bias, ln1_g, ln1_b, f1_w1, f1_w3, f1_w2, w_in, b_in, sgu_ln_g, sgu_ln_b, sgu_ws, sgu_b, w_oa, w_ob, w_out, ln2_g, ln2_b, f2_w1, f2_w3, f2_w2, ln3_g, ln3_b):
    depth = ln1_g.shape[0]
    alpha = (2 * depth) ** 0.25
    batch, seq, _ = x_prompt.shape
    dec_batch, dec_seq, _ = x_sample.shape
    states = (state_win0, state_win1, state_win2)

    yp = x_prompt.reshape(batch * seq, D_MODEL)
    ys = x_sample.reshape(dec_batch * dec_seq, D_MODEL)
    outs = [[] for _ in range(7)]
    for l in range(depth):
        f1 = _prep_ffn(f1_w1[l], f1_w3[l], f1_w2[l], ln1_g[l], ln1_b[l])
        f2 = _prep_ffn(f2_w1[l], f2_w3[l], f2_w2[l], ln3_g[l], ln3_b[l])
        wqkv, bqkv, wuv, buv, wg, bg = _prep_w_in(w_in[l], b_in[l])
        sln = (sgu_ln_g[l][None, :], sgu_ln_b[l][None, :])
        woa, wob, wout = w_oa[l].astype(BF16), w_ob[l].astype(BF16), w_out[l].astype(BF16)
        ln2 = (ln2_g[l][None, :], ln2_b[l][None, :])

        hs, _ = _ffn_ln(ys, f1, alpha)
        ws4 = sgu_ws[l][:, :dec_seq, :dec_seq].reshape(SGU_GROUPS, dec_seq * dec_seq)
        sb4 = sgu_b[l][:, :dec_seq]
        qs, kvs, ybs, sgs, vn = _proj_sample(hs, dec_seq, ws4, sb4, wqkv, bqkv, wuv, buv, wg, bg, *sln)
        nt = seq // TM
        n_steps = batch * nt
        sides = []
        for g in range(N_GROUPS):
            st = states[g][l]
            lg = st.shape[1]
            st_cm = jnp.transpose(st, (0, 2, 3, 4, 1)).reshape(dec_batch, 2 * GW, lg)
            q16 = jnp.tile(qs[:, g * GW:(g + 1) * GW].reshape(dec_batch, 1, dec_seq, GW),
                           (1, GROUP_HEADS, 1, 1)).reshape(dec_batch, GROUP_HEADS * dec_seq, GW)
            kvnew = kvs[:, g * 2 * GW:(g + 1) * 2 * GW].reshape(dec_batch, dec_seq, 2 * GW)
            step_of = (lambda i: i) if g == N_GROUPS - 1 else (lambda b, t: b * nt + t)
            sides.append(_attn_sample_side(g, rel_bias, q16, kvnew, st_cm, n_steps, step_of))

        hp, side_a = _ffn_ln(yp, f1, alpha, sides=sides[2:])
        res, side_b = _proj_prompt(hp, batch, seq, wqkv, bqkv, wuv, buv, wg, bg, *sln,
                                   sgu_ws[l], jnp.transpose(sgu_b[l]), sides=sides[:2])
        qkvs, wins, ybp, sgp = res[0:3], res[3:6], res[6], res[7]
        ols = _attn_prompt(rel_bias, qkvs, batch)
        yp = _mix_ffn(hp, ols, ybp, sgp, woa, wob, wout, *ln2, f2, alpha, seq)
        for g in range(N_GROUPS):
            outs[g].append(_to_window(wins[g]))

        ols_s = []
        for g, (win_cm, ol) in enumerate(side_b + side_a):
            lg = win_cm.shape[2]
            win = jnp.transpose(win_cm.reshape(dec_batch, 2, GROUP_HEADS, HEAD_DIM, lg), (0, 4, 1, 2, 3))
            outs[3 + g].append(win)
            ols_s.append(ol.reshape(1, 1, dec_batch * dec_seq, 2 * GW))
        ys = _mix_ffn(hs, ols_s, ybs, sgs, woa, wob, wout, *ln2, f2, alpha, dec_batch * dec_seq)
        outs[6].append(vn.reshape(dec_batch, dec_seq, SGU_W))

    return (yp.reshape(batch, seq, D_MODEL), ys.reshape(dec_batch, dec_seq, D_MODEL),
            *[jnp.stack(o) for o in outs])
```

```python
import functools
import math
from typing import Callable, NamedTuple

import jax
import jax.numpy as jnp
import numpy as np
from jax import lax
from jax.experimental import pallas as pl
from jax.experimental.pallas import tpu as pltpu

F32 = jnp.float32
BF16 = jnp.bfloat16

D_MODEL = 1024
HEAD_DIM = 64
GROUP_HEADS = 4
ATT_GROUPS = ((128, 1), (512, 4), (2048, 16))
N_GROUPS = len(ATT_GROUPS)
N_ATT_HEADS = N_GROUPS * GROUP_HEADS
ATT_W = N_ATT_HEADS * HEAD_DIM
GW = GROUP_HEADS * HEAD_DIM
BLK = 128
N_BUCKETS = 32
MAX_DISTANCE = 2048
SGU_CHUNK = 128
SGU_GROUPS = 4
SGU_GROUP_W = 128
SGU_W = SGU_GROUPS * SGU_GROUP_W
D_FF = 2816
LN_EPS = 1e-5
NEG = -1e30
SCALE = HEAD_DIM ** -0.5

TM = 512
FC = 256
NC = D_FF // FC
LN_PARTS = 2
ATTN_UNROLL = 16
ATTN_UNROLL_SINGLE = 4
VMEM_LIMIT = 60 * 1024 * 1024


def _t5_bucket(dist):
    dist = np.asarray(dist, np.int64)
    max_exact = N_BUCKETS // 2
    large = max_exact + (np.log(np.maximum(dist, max_exact) / max_exact)
                         / np.log(MAX_DISTANCE / max_exact) * (N_BUCKETS - max_exact)).astype(np.int64)
    large = np.minimum(large, N_BUCKETS - 1)
    return np.where(dist < max_exact, dist, large).astype(np.int32)


def _ln(t, g, b):
    mu = jnp.mean(t, axis=-1, keepdims=True)
    d = t - mu
    var = jnp.mean(d * d, axis=-1, keepdims=True)
    return d * lax.rsqrt(var + LN_EPS) * g + b


def _whole(shape):
    nd = len(shape)
    return pl.BlockSpec(shape, lambda *_: (0,) * nd, pipeline_mode=pl.Buffered(1))


class _Side(NamedTuple):
    name: str
    init: Callable
    step: Callable
    inputs: tuple
    in_specs: tuple
    out_shape: tuple
    out_specs: tuple
    scratch_shapes: tuple


def _take(refs, pos, count):
    return refs[pos:pos + count], pos + count


def _call(name, main, main_init, grid, inputs, in_specs, out_shape, out_specs, scratch, sides=()):
    n_in, n_out, n_scr = len(inputs), len(out_shape), len(scratch)

    def body(*refs):
        refs = list(refs)
        m_in, pos = _take(refs, 0, n_in)
        s_in = []
        for s in sides:
            r, pos = _take(refs, pos, len(s.inputs))
            s_in.append(r)
        m_out, pos = _take(refs, pos, n_out)
        s_out = []
        for s in sides:
            r, pos = _take(refs, pos, len(s.out_shape))
            s_out.append(r)
        m_scr, pos = _take(refs, pos, n_scr)
        s_scr = []
        for s in sides:
            r, pos = _take(refs, pos, len(s.scratch_shapes))
            s_scr.append(r)

        if main_init is not None or sides:
            first = pl.program_id(0) == 0
            for axis in range(1, len(grid)):
                first = jnp.logical_and(first, pl.program_id(axis) == 0)

            @pl.when(first)
            def _():
                if main_init is not None:
                    main_init(m_in, m_out, m_scr)
                for s, i, o, c in zip(sides, s_in, s_out, s_scr):
                    s.init(i, o, c)

        main(m_in, m_out, m_scr)
        for s, i, o, c in zip(sides, s_in, s_out, s_scr):
            s.step(i, o, c)

    all_in = list(inputs) + [a for s in sides for a in s.inputs]
    all_in_specs = list(in_specs) + [a for s in sides for a in s.in_specs]
    all_out_shape = list(out_shape) + [a for s in sides for a in s.out_shape]
    all_out_specs = list(out_specs) + [a for s in sides for a in s.out_specs]
    all_scratch = list(scratch) + [a for s in sides for a in s.scratch_shapes]
    res = pl.pallas_call(
        body,
        grid=grid,
        in_specs=all_in_specs,
        out_specs=all_out_specs,
        out_shape=all_out_shape,
        scratch_shapes=all_scratch,
        compiler_params=pltpu.CompilerParams(dimension_semantics=("arbitrary",) * len(grid),
                                             vmem_limit_bytes=VMEM_LIMIT),
        name=name + "".join("_" + s.name for s in sides),
    )(*all_in)
    res = list(res)
    main_res, pos = _take(res, 0, n_out)
    side_res = []
    for s in sides:
        r, pos = _take(res, pos, len(s.out_shape))
        side_res.append(r)
    return main_res, side_res


def _swiglu_ln(alpha, x_ref, xb_ref, w_refs, hid_ref, o_ref):
    w1_ref, w3_ref, w2_ref, g_ref, b_ref = w_refs
    for c in range(NC):
        cols = slice(c * FC, (c + 1) * FC)
        a = jnp.dot(xb_ref[...], w1_ref[:, cols], preferred_element_type=F32)
        b = jnp.dot(xb_ref[...], w3_ref[:, cols], preferred_element_type=F32)
        hid_ref[:, cols] = ((a * jax.nn.sigmoid(a)) * b).astype(BF16)
    tm = x_ref.shape[0]
    for rows in [slice(r, r + tm // LN_PARTS) for r in range(0, tm, tm // LN_PARTS)]:
        ffn = jnp.dot(hid_ref[rows, :], w2_ref[...], preferred_element_type=F32)
        t = alpha * x_ref[rows, :] + 0.5 * ffn
        o_ref[rows, :] = _ln(t, g_ref[...], b_ref[...])


N_FFN_W = 5


def _ffn_specs():
    return [_whole((D_MODEL, D_FF)), _whole((D_MODEL, D_FF)), _whole((D_FF, D_MODEL)),
            _whole((1, D_MODEL)), _whole((1, D_MODEL))]


def _ffn_ln_main(alpha, in_refs, out_refs, scratch_refs):
    x_ref = in_refs[0]
    xb_ref, hid_ref = scratch_refs
    xb_ref[...] = x_ref[...].astype(BF16)
    _swiglu_ln(alpha, x_ref, xb_ref, in_refs[1:1 + N_FFN_W], hid_ref, out_refs[0])


def _ffn_ln(x2d, ffn_w, alpha, sides=()):
    n = x2d.shape[0]
    assert n % TM == 0 and len(ffn_w) == N_FFN_W
    (out,), side_res = _call(
        "ffn_ln", functools.partial(_ffn_ln_main, alpha), None, (n // TM,),
        [x2d, *ffn_w],
        [pl.BlockSpec((TM, D_MODEL), lambda i: (i, 0))] + _ffn_specs(),
        [jax.ShapeDtypeStruct((n, D_MODEL), F32)],
        [pl.BlockSpec((TM, D_MODEL), lambda i: (i, 0))],
        [pltpu.VMEM((TM, D_MODEL), BF16), pltpu.VMEM((TM, D_FF), BF16)],
        sides)
    return out, side_res


def _gelu_exact(z):
    return 0.5 * z * (1.0 + lax.erf(z * (1.0 / math.sqrt(2.0))))


def _uv_branch(hb, wuv_ref, buv_ref, sg_ref, sb_ref):
    zuv = jnp.dot(hb, wuv_ref[...], preferred_element_type=F32) + buv_ref[...]
    zz = _gelu_exact(zuv)
    u = zz[:, :SGU_W]
    vn = _ln(zz[:, SGU_W:], sg_ref[...], sb_ref[...])
    return u, vn


def _gates(hb, wg_ref, bg_ref, sg_out_ref):
    zg = jnp.dot(hb, wg_ref[...], preferred_element_type=F32) + bg_ref[...]
    sg_out_ref[...] = jax.nn.sigmoid(zg).astype(BF16)


def _proj_prompt_main(in_refs, out_refs, scratch_refs):
    (h_ref, wqkv_ref, bqkv_ref, wuv_ref, buv_ref, wg_ref, bg_ref,
     sln_g_ref, sln_b_ref, ws_ref, sbt_ref) = in_refs
    qkv_refs, win_refs = out_refs[0:3], out_refs[3:6]
    yb_ref, sg_ref = out_refs[6:8]
    zs_ref, hb_ref = scratch_refs
    hb_ref[...] = h_ref[...].astype(BF16)
    u, vn = _uv_branch(hb_ref[...], wuv_ref, buv_ref, sln_g_ref, sln_b_ref)
    _gates(hb_ref[...], wg_ref, bg_ref, sg_ref)

    vnb = vn.astype(BF16)
    row = lax.broadcasted_iota(jnp.int32, (SGU_CHUNK, SGU_CHUNK), 0)
    col = lax.broadcasted_iota(jnp.int32, (SGU_CHUNK, SGU_CHUNK), 1)
    for g in range(SGU_GROUPS):
        wm = jnp.where(row >= col, ws_ref[g], 0.0).astype(BF16)
        bias = jnp.broadcast_to(sbt_ref[:, g:g + 1], (SGU_CHUNK, SGU_GROUP_W))
        cs = slice(g * SGU_GROUP_W, (g + 1) * SGU_GROUP_W)
        for c in range(TM // SGU_CHUNK):
            rs = slice(c * SGU_CHUNK, (c + 1) * SGU_CHUNK)
            mixed = jnp.dot(wm, vnb[rs, cs], preferred_element_type=F32) + bias
            yb_ref[rs, cs] = (u[rs, cs] * mixed).astype(BF16)

    for g, (window, dil) in enumerate(ATT_GROUPS):
        zg = jnp.dot(hb_ref[...], wqkv_ref[g], preferred_element_type=F32) + bqkv_ref[g]
        keep = min(window, TM)
        win_refs[g][0] = zg[TM - keep:, GW:]
        zq = jnp.concatenate([zg[:, :GW] * SCALE, zg[:, GW:]], axis=1)
        if dil == 1:
            qkv_refs[g][0, 0] = zq.astype(BF16)
            continue
        ncol = 3 * GW // BLK
        for j in range(ncol):
            zs_ref[j] = zq[:, j * BLK:(j + 1) * BLK]
        rows = TM // dil
        for r in range(dil):
            sub = jnp.concatenate([zs_ref[j, pl.ds(r, rows, stride=dil), :] for j in range(ncol)], axis=1)
            qkv_refs[g][0, r] = sub.astype(BF16)


def _proj_prompt(h2d, batch, seq, wqkv, bqkv, wuv, buv, wg, bg, sln_g, sln_b, ws, sbt, sides=()):
    nt = seq // TM
    assert seq % TM == 0 and TM == ATT_GROUPS[1][0] and TM >= ATT_GROUPS[0][0]
    ins = [h2d, wqkv, bqkv, wuv, buv, wg, bg, sln_g, sln_b, ws, sbt]
    in_specs = [pl.BlockSpec((TM, D_MODEL), lambda b, t: (b * nt + t, 0))] + [_whole(a.shape) for a in ins[1:]]
    out_shape, out_specs = [], []
    for window, dil in ATT_GROUPS:
        out_shape.append(jax.ShapeDtypeStruct((batch, dil, seq // dil, 3 * GW), BF16))
        out_specs.append(pl.BlockSpec((1, dil, TM // dil, 3 * GW), lambda b, t: (b, 0, t, 0)))
    for window, dil in ATT_GROUPS:
        keep = min(window, seq)
        out_shape.append(jax.ShapeDtypeStruct((batch, keep, 2 * GW), F32))
        if keep > TM:
            assert keep == seq
            out_specs.append(pl.BlockSpec((1, TM, 2 * GW), lambda b, t: (b, t, 0)))
        else:
            out_specs.append(pl.BlockSpec((1, keep, 2 * GW), lambda b, t: (b, 0, 0)))
    out_shape += [jax.ShapeDtypeStruct((batch * seq, SGU_W), BF16),
                  jax.ShapeDtypeStruct((batch * seq, 2 * D_MODEL), BF16)]
    out_specs += [pl.BlockSpec((TM, SGU_W), lambda b, t: (b * nt + t, 0)),
                  pl.BlockSpec((TM, 2 * D_MODEL), lambda b, t: (b * nt + t, 0))]
    return _call("proj_prompt", _proj_prompt_main, None, (batch, nt), ins, in_specs, out_shape, out_specs,
                 [pltpu.VMEM((3 * GW // BLK, TM, BLK), F32), pltpu.VMEM((TM, D_MODEL), BF16)], sides)


def _proj_sample_kernel(t_len, ws4_ref, sb4_ref, h_ref, wqkv_ref, bqkv_ref, wuv_ref, buv_ref,
                        wg_ref, bg_ref, sln_g_ref, sln_b_ref,
                        q_ref, kv_ref, yb_ref, sg_ref, vn_ref):
    n = h_ref.shape[0]
    hb = h_ref[...].astype(BF16)
    for g in range(N_GROUPS):
        zg = jnp.dot(hb, wqkv_ref[g], preferred_element_type=F32) + bqkv_ref[g]
        q_ref[:, g * GW:(g + 1) * GW] = zg[:, :GW] * SCALE
        kv_ref[:, g * 2 * GW:(g + 1) * 2 * GW] = zg[:, GW:]

    u, vn = _uv_branch(hb, wuv_ref, buv_ref, sln_g_ref, sln_b_ref)
    vn_ref[...] = vn
    p = lax.broadcasted_iota(jnp.int32, (n, SGU_GROUP_W), 0) & (t_len - 1)
    for g in range(SGU_GROUPS):
        cs = slice(g * SGU_GROUP_W, (g + 1) * SGU_GROUP_W)
        vg = vn[:, cs]
        acc = jnp.zeros((n, SGU_GROUP_W), F32)
        for pp in range(t_len):
            acc = jnp.where(p == pp, sb4_ref[g, pp], acc)
        for d in range(t_len):
            coef = jnp.zeros((n, SGU_GROUP_W), F32)
            for pp in range(d, t_len):
                coef = jnp.where(p == pp, ws4_ref[g, pp * t_len + (pp - d)], coef)
            shifted = vg if d == 0 else pltpu.roll(vg, d, axis=0)
            acc = acc + coef * shifted
        yb_ref[:, cs] = (u[:, cs] * acc).astype(BF16)

    _gates(hb, wg_ref, bg_ref, sg_ref)


def _proj_sample(h2d, t_len, ws4, sb4, wqkv, bqkv, wuv, buv, wg, bg, sln_g, sln_b):
    n = h2d.shape[0]
    assert t_len & (t_len - 1) == 0
    smem = pl.BlockSpec(memory_space=pltpu.SMEM)
    ins = (h2d, wqkv, bqkv, wuv, buv, wg, bg, sln_g, sln_b)
    out_shape = [jax.ShapeDtypeStruct((n, N_GROUPS * GW), F32),
                 jax.ShapeDtypeStruct((n, N_GROUPS * 2 * GW), F32),
                 jax.ShapeDtypeStruct((n, SGU_W), BF16),
                 jax.ShapeDtypeStruct((n, 2 * D_MODEL), BF16),
                 jax.ShapeDtypeStruct((n, SGU_W), F32)]
    return pl.pallas_call(
        functools.partial(_proj_sample_kernel, t_len),
        grid=(1,),
        in_specs=[smem, smem] + [_whole(a.shape) for a in ins],
        out_specs=[_whole(s.shape) for s in out_shape],
        out_shape=out_shape,
        compiler_params=pltpu.CompilerParams(dimension_semantics=("arbitrary",), vmem_limit_bytes=VMEM_LIMIT),
        name="proj_sample",
    )(ws4, sb4, *ins)


def _bias_mask(bmap, buckets, value_of_bucket):
    acc = jnp.full(bmap.shape, NEG, F32)
    for bkt in buckets:
        acc = jnp.where(bmap == bkt, value_of_bucket(bkt), acc)
    return acc


def _prompt_bucket_map(window, dil):
    band = window // dil
    rel = np.arange(BLK)[:, None] - np.arange(2 * BLK)[None, :] + BLK
    valid = (rel >= 0) & (rel <= band)
    return np.where(valid, _t5_bucket(np.clip(rel, 0, band) * dil), -1).astype(np.int32)


def _combine_groups(outs, lses):
    mx = jnp.maximum(jnp.maximum(lses[0], lses[1]), lses[2])
    es = [jnp.exp(ls - mx) for ls in lses]
    den = es[0] + es[1] + es[2]
    num = es[0] * outs[0] + es[1] * outs[1] + es[2] * outs[2]
    return num * (1.0 / den)


def _attn_prompt_kernel(buckets, tab_ref, bmap_ref, qkv0_ref, qkv1_ref, qkv2_ref,
                        ol0_ref, ol1_ref, ol2_ref, bm_ref):
    @pl.when(pl.program_id(0) == 0)
    def _():
        for g in range(N_GROUPS):
            bmap = bmap_ref[g]
            for h in range(GROUP_HEADS):
                rs = slice(h * BLK, (h + 1) * BLK)
                bm = _bias_mask(bmap, buckets[g], lambda bkt: tab_ref[bkt, g * GROUP_HEADS + h])
                bm_ref[2 * g, rs, :] = bm
                bm_ref[2 * g + 1, rs, :BLK] = bm[:, BLK:]
                bm_ref[2 * g + 1, rs, BLK:] = jnp.full((BLK, BLK), NEG, F32)

    lane_head = lax.broadcasted_iota(jnp.int32, (BLK, GW), 1) >> 6
    lane_lo = lax.broadcasted_iota(jnp.int32, (BLK, BLK), 1) < HEAD_DIM

    def unit(g, qkv_ref, ol_ref, r, n):
        single = qkv_ref.shape[2] == BLK
        static_n = isinstance(n, int)
        q0 = n * BLK if static_n else pl.multiple_of(n * BLK, BLK)
        q = qkv_ref[0, r, pl.ds(q0, BLK), 0:GW]
        qs = jnp.concatenate([jnp.where(lane_head == h, q, jnp.zeros_like(q))
                              for h in range(GROUP_HEADS)], axis=0)
        if single:
            k = qkv_ref[0, r, :, GW:2 * GW]
            v = qkv_ref[0, r, :, 2 * GW:3 * GW]
            bm = bm_ref[2 * g, :, BLK:]
        else:
            if static_n:
                k0 = max(q0 - BLK, 0)
                bm = bm_ref[2 * g + int(n == 0)]
            else:
                k0 = pl.multiple_of(jnp.maximum(q0 - BLK, 0), BLK)
                bm = bm_ref[2 * g + (n == 0).astype(jnp.int32)]
            k = qkv_ref[0, r, pl.ds(k0, 2 * BLK), GW:2 * GW]
            v = qkv_ref[0, r, pl.ds(k0, 2 * BLK), 2 * GW:3 * GW]
        s = lax.dot_general(qs, k, (((1,), (1,)), ((), ())), preferred_element_type=F32) + bm
        m = jnp.max(s, axis=-1, keepdims=True)
        p = jnp.exp(s - m)
        l = jnp.sum(p, axis=-1, keepdims=True)
        pb = p.astype(BF16)
        linv = 1.0 / l
        lse = m + jnp.log(l)
        rows = pl.ds(q0, BLK)
        for half in range(2):
            pv = jnp.dot(pb[2 * half * BLK:(2 * half + 2) * BLK], v[:, half * BLK:(half + 1) * BLK],
                         preferred_element_type=F32)
            ra = slice(2 * half * BLK, (2 * half + 1) * BLK)
            rb = slice((2 * half + 1) * BLK, (2 * half + 2) * BLK)
            ol_ref[0, r, rows, half * BLK:(half + 1) * BLK] = jnp.where(
                lane_lo, pv[:BLK] * linv[ra], pv[BLK:] * linv[rb])
            ol_ref[0, r, rows, GW + half * BLK:GW + (half + 1) * BLK] = jnp.where(lane_lo, lse[ra], lse[rb])

    for g, (qkv_ref, ol_ref) in enumerate(((qkv0_ref, ol0_ref), (qkv1_ref, ol1_ref), (qkv2_ref, ol2_ref))):
        dil, sub_len = qkv_ref.shape[1], qkv_ref.shape[2]
        nb = sub_len // BLK
        n_units = dil * nb
        unroll = ATTN_UNROLL_SINGLE if nb == 1 else ATTN_UNROLL
        assert n_units % unroll == 0 and (nb == 1 or unroll % nb == 0 or dil == 1)

        def trip(t, carry, g=g, qkv_ref=qkv_ref, ol_ref=ol_ref, nb=nb, dil=dil, unroll=unroll):
            for j in range(unroll):
                if nb == 1:
                    r, n = t * unroll + j, 0
                elif dil == 1:
                    r, n = 0, t * unroll + j
                else:
                    r, n = t * (unroll // nb) + j // nb, j % nb
                unit(g, qkv_ref, ol_ref, r, n)
            return carry

        lax.fori_loop(0, n_units // unroll, trip, 0)


def _attn_prompt(tab, qkvs, batch):
    bmaps = [_prompt_bucket_map(w, d) for w, d in ATT_GROUPS]
    buckets = tuple(tuple(int(b) for b in np.unique(m) if b >= 0) for m in bmaps)
    bmap = jnp.asarray(np.stack(bmaps))
    in_specs = [pl.BlockSpec(memory_space=pltpu.SMEM), _whole(bmap.shape)]
    out_shape, out_specs = [], []
    for a in qkvs:
        _, dil, sub_len, _ = a.shape
        in_specs.append(pl.BlockSpec((1, dil, sub_len, 3 * GW), lambda b: (b, 0, 0, 0)))
        out_shape.append(jax.ShapeDtypeStruct((batch, dil, sub_len, 2 * GW), F32))
        out_specs.append(pl.BlockSpec((1, dil, sub_len, 2 * GW), lambda b: (b, 0, 0, 0)))
    return pl.pallas_call(
        functools.partial(_attn_prompt_kernel, buckets),
        grid=(batch,),
        in_specs=in_specs,
        out_specs=out_specs,
        out_shape=out_shape,
        scratch_shapes=[pltpu.VMEM((2 * N_GROUPS, GROUP_HEADS * BLK, 2 * BLK), F32)],
        compiler_params=pltpu.CompilerParams(dimension_semantics=("arbitrary",), vmem_limit_bytes=VMEM_LIMIT),
        name="attn_prompt",
    )(tab, bmap, *qkvs)


def _attn_sample_init(g, t_len, buckets, in_refs, out_refs, scratch_refs):
    tab_ref, bmap_ref = in_refs[:2]
    bm_ref, tr_ref, _ = scratch_refs
    rh =lax.broadcasted_iota(jnp.int32, bm_ref.shape, 0) // t_len

    def value(bkt):
        v = jnp.full(bm_ref.shape, tab_ref[bkt, g * GROUP_HEADS], F32)
        for h in range(1, GROUP_HEADS):
            v = jnp.where(rh == h, tab_ref[bkt, g * GROUP_HEADS + h], v)
        return v

    bm_ref[...] = _bias_mask(bmap_ref[...], buckets, value)
    tr_ref[...] = jnp.zeros_like(tr_ref)


def _attn_sample_step(t_len, in_refs, out_refs, scratch_refs):
    _, _, q_ref, kvnew_ref, st_ref = in_refs
    win_ref, ol_ref = out_refs
    bm_ref, tr_ref, fb_ref = scratch_refs
    lg = st_ref.shape[2]
    nrow = GROUP_HEADS * t_len
    row_head = lax.broadcasted_iota(jnp.int32, (nrow, GW), 0) // t_len
    lane_head = lax.broadcasted_iota(jnp.int32, (nrow, GW), 1) >> 6
    ext = lg + BLK
    sel = row_head == lane_head
    for i in range(st_ref.shape[0]):
        tr_ref[i, 0:t_len, :] = kvnew_ref[i]
        for c in range(2 * GW // BLK):
            rows = slice(c * BLK, (c + 1) * BLK)
            full = jnp.concatenate([st_ref[i, rows, :], tr_ref[i, :, rows].T], axis=1)
            win_ref[i, rows, :] = pltpu.roll(full, ext - t_len, axis=1)[:, :lg]
            fb_ref[i, rows, :] = full.astype(BF16)

        qs = jnp.where(sel, q_ref[i], 0.0).astype(BF16)
        s = jnp.dot(qs, fb_ref[i, 0:GW, :], preferred_element_type=F32) + bm_ref[...]
        m = jnp.max(s, axis=-1, keepdims=True)
        p = jnp.exp(s - m)
        l = jnp.sum(p, axis=-1, keepdims=True)
        pv = lax.dot_general(p.astype(BF16), fb_ref[i, GW:2 * GW, :], (((1,), (1,)), ((), ())),
                             preferred_element_type=F32) * (1.0 / l)
        lse = jnp.broadcast_to(m + jnp.log(l), (nrow, GW))
        o = jnp.where(sel, pv, 0.0)
        ls = jnp.where(sel, lse, 0.0)
        osum, lsum = o, ls
        for h in range(1, GROUP_HEADS):
            osum = osum + pltpu.roll(o, h * t_len, axis=0)
            lsum = lsum + pltpu.roll(ls, h * t_len, axis=0)
        ol_ref[i, :, 0:GW] = osum[0:t_len]
        ol_ref[i, :, GW:2 * GW] = lsum[0:t_len]


def _sample_bucket_map(window, dil, lg, t_len):
    band = window // dil
    ext = lg + BLK
    bmap = np.full((t_len, ext), -1, np.int64)
    t = np.arange(t_len)[:, None]
    j = np.arange(lg + t_len)[None, :]
    dist = lg + t - j
    valid = (dist >= 0) & (dist % dil == 0) & (dist <= band * dil)
    bmap[:, :lg + t_len] = np.where(valid, _t5_bucket(np.maximum(dist, 0)), -1)
    return np.tile(bmap, (GROUP_HEADS, 1)).astype(np.int32)


def _attn_sample_side(g, tab, q16, kvnew, st, n_steps, step_of):
    window, dil = ATT_GROUPS[g]
    nb, _, lg = st.shape
    assert lg == window and nb % n_steps == 0
    per = nb // n_steps
    t_len = kvnew.shape[1]
    nrow = GROUP_HEADS * t_len
    bmap_np = _sample_bucket_map(window, dil, lg, t_len)
    buckets = tuple(int(b) for b in np.unique(bmap_np) if b >= 0)
    bmap = jnp.asarray(bmap_np)

    def blk(*idx):
        return (step_of(*idx), 0, 0)

    return _Side(
        name="attn_sample_g%d" % g,
        init=functools.partial(_attn_sample_init, g, t_len, buckets),
        step=functools.partial(_attn_sample_step, t_len),
        inputs=(tab, bmap, q16, kvnew, st),
        in_specs=(pl.BlockSpec(memory_space=pltpu.SMEM), _whole(bmap.shape),
                  pl.BlockSpec((per, nrow, GW), blk),
                  pl.BlockSpec((per, t_len, 2 * GW), blk),
                  pl.BlockSpec((per, 2 * GW, lg), blk)),
        out_shape=(jax.ShapeDtypeStruct((nb, 2 * GW, lg), F32),
                   jax.ShapeDtypeStruct((nb, t_len, 2 * GW), F32)),
        out_specs=(pl.BlockSpec((per, 2 * GW, lg), blk),
                   pl.BlockSpec((per, t_len, 2 * GW), blk)),
        scratch_shapes=(pltpu.VMEM((nrow, lg + BLK), F32), pltpu.VMEM((per, BLK, 2 * GW), F32),
                        pltpu.VMEM((per, 2 * GW, lg + BLK), BF16)))


def _mix_ffn_main(alpha, in_refs, out_refs, scratch_refs):
    h_ref = in_refs[0]
    ol_refs = in_refs[1:1 + N_GROUPS]
    yb_ref, sg_ref, woa_ref, wob_ref, wout_ref, g2_ref, b2_ref = in_refs[1 + N_GROUPS:8 + N_GROUPS]
    ffn_refs = in_refs[8 + N_GROUPS:]
    (o_ref,) = out_refs
    u_ref, h2_ref, h2b_ref, hid_ref = scratch_refs
    tm = h_ref.shape[0]
    ncol = 2 * GW // BLK
    ols = []
    for gi, ol_ref in enumerate(ol_refs):
        dil = ol_ref.shape[1]
        if dil == 1:
            ols.append(ol_ref[0, 0])
            continue
        for r in range(dil):
            for j in range(ncol):
                u_ref[gi, j, pl.ds(r, tm // dil, stride=dil), :] = ol_ref[0, r, :, j * BLK:(j + 1) * BLK]
        ols.append(jnp.concatenate([u_ref[gi, j] for j in range(ncol)], axis=1))
    att = _combine_groups([ol[:, :GW] for ol in ols], [ol[:, GW:] for ol in ols]).astype(BF16)
    a = jnp.dot(att, woa_ref[...], preferred_element_type=F32)
    bb = jnp.dot(yb_ref[...], wob_ref[...], preferred_element_type=F32)
    gated = sg_ref[:, :D_MODEL].astype(F32) * a + sg_ref[:, D_MODEL:].astype(F32) * bb
    mix = jnp.dot(gated.astype(BF16), wout_ref[...], preferred_element_type=F32)
    h2 = _ln(alpha * h_ref[...] + mix, g2_ref[...], b2_ref[...])
    h2_ref[...] = h2
    h2b_ref[...] = h2.astype(BF16)
    _swiglu_ln(alpha, h2_ref, h2b_ref, ffn_refs, hid_ref, o_ref)


def _mix_ffn(h2d, ols, yb, sg, woa, wob, wout, g2, b2, ffn_w, alpha, seq):
    n = h2d.shape[0]
    assert n % TM == 0 and seq % TM == 0 and len(ffn_w) == N_FFN_W
    nt = seq // TM

    def rows(a):
        return pl.BlockSpec((TM, a.shape[1]), lambda i: (i, 0))

    ins = [h2d, *ols, yb, sg, woa, wob, wout, g2, b2, *ffn_w]
    in_specs = [rows(h2d)]
    for a in ols:
        dil = a.shape[1]
        in_specs.append(pl.BlockSpec((1, dil, TM // dil, 2 * GW), lambda i: (i // nt, 0, i % nt, 0)))
    in_specs += ([rows(yb), rows(sg)] + [_whole(a.shape) for a in (woa, wob, wout, g2, b2)] + _ffn_specs())
    (out,), _ = _call(
        "mix_ffn", functools.partial(_mix_ffn_main, alpha), None, (n // TM,), ins, in_specs,
        [jax.ShapeDtypeStruct((n, D_MODEL), F32)], [rows(h2d)],
        [pltpu.VMEM((N_GROUPS, 2 * GW // BLK, TM, BLK), F32), pltpu.VMEM((TM, D_MODEL), F32),
         pltpu.VMEM((TM, D_MODEL), BF16), pltpu.VMEM((TM, D_FF), BF16)])
    return out


def _prep_ffn(w1, w3, w2, ln_g, ln_b):
    return (w1.astype(BF16), w3.astype(BF16), w2.astype(BF16), ln_g[None, :], ln_b[None, :])


def _prep_w_in(w_in, b_in):
    def cols(a, lo, n):
        return a[..., lo:lo + n]
    wq, bq = [], []
    for g in range(N_GROUPS):
        parts = [(i * ATT_W + g * GW, GW) for i in range(3)]
        wq.append(jnp.concatenate([cols(w_in, lo, n) for lo, n in parts], axis=-1))
        bq.append(jnp.concatenate([cols(b_in, lo, n) for lo, n in parts], axis=-1))
    o_uv = 3 * ATT_W
    return (jnp.stack(wq).astype(BF16), jnp.stack(bq)[:, None, :],
            cols(w_in, o_uv, 2 * SGU_W).astype(BF16), cols(b_in, o_uv, 2 * SGU_W)[None, :],
            cols(w_in, o_uv + 2 * SGU_W, 2 * D_MODEL).astype(BF16),
            cols(b_in, o_uv + 2 * SGU_W, 2 * D_MODEL)[None, :])


def _to_window(a):
    return a.reshape(a.shape[0], a.shape[1], 2, GROUP_HEADS, HEAD_DIM)


def kernel(x_prompt, x_sample, state_win0, state_win1, state_win2, rel_bias, ln1_g, ln1_b, f1_w1, f1_w3, f1_w2, w_in, b_in, sgu_ln_g, sgu_ln_b, sgu_ws, sgu_b, w_oa, w_ob, w_out, ln2_g, ln2_b, f2_w1, f2_w3, f2_w2, ln3_g, ln3_b):
    depth = ln1_g.shape[0]
    alpha = (2 * depth) ** 0.25
    batch, seq, _ = x_prompt.shape
    dec_batch, dec_seq, _ = x_sample.shape
    states = (state_win0, state_win1, state_win2)

    yp = x_prompt.reshape(batch * seq, D_MODEL)
    ys = x_sample.reshape(dec_batch * dec_seq, D_MODEL)
    outs = [[] for _ in range(7)]
    for l in range(depth):
        f1 = _prep_ffn(f1_w1[l], f1_w3[l], f1_w2[l], ln1_g[l], ln1_b[l])
        f2 = _prep_ffn(f2_w1[l], f2_w3[l], f2_w2[l], ln3_g[l], ln3_b[l])
        wqkv, bqkv, wuv, buv, wg, bg = _prep_w_in(w_in[l], b_in[l])
        sln = (sgu_ln_g[l][None, :], sgu_ln_b[l][None, :])
        woa, wob, wout = w_oa[l].astype(BF16), w_ob[l].astype(BF16), w_out[l].astype(BF16)
        ln2 = (ln2_g[l][None, :], ln2_b[l][None, :])

        hs, _ = _ffn_ln(ys, f1, alpha)
        ws4 = sgu_ws[l][:, :dec_seq, :dec_seq].reshape(SGU_GROUPS, dec_seq * dec_seq)
        sb4 = sgu_b[l][:, :dec_seq]
        qs, kvs, ybs, sgs, vn = _proj_sample(hs, dec_seq, ws4, sb4, wqkv, bqkv, wuv, buv, wg, bg, *sln)
        nt = seq // TM
        n_steps = batch * nt
        sides = []
        for g in range(N_GROUPS):
            st = states[g][l]
            lg = st.shape[1]
            st_cm = jnp.transpose(st, (0, 2, 3, 4, 1)).reshape(dec_batch, 2 * GW, lg)
            q16 = jnp.tile(qs[:, g * GW:(g + 1) * GW].reshape(dec_batch, 1, dec_seq, GW),
                           (1, GROUP_HEADS, 1, 1)).reshape(dec_batch, GROUP_HEADS * dec_seq, GW)
            kvnew = kvs[:, g * 2 * GW:(g + 1) * 2 * GW].reshape(dec_batch, dec_seq, 2 * GW)
            step_of = (lambda i: i) if g == N_GROUPS - 1 else (lambda b, t: b * nt + t)
            sides.append(_attn_sample_side(g, rel_bias, q16, kvnew, st_cm, n_steps, step_of))

        hp, side_a = _ffn_ln(yp, f1, alpha, sides=sides[2:])
        res, side_b = _proj_prompt(hp, batch, seq, wqkv, bqkv, wuv, buv, wg, bg, *sln,
                                   sgu_ws[l], jnp.transpose(sgu_b[l]), sides=sides[:2])
        qkvs, wins, ybp, sgp = res[0:3], res[3:6], res[6], res[7]
        ols = _attn_prompt(rel_bias, qkvs, batch)
        yp = _mix_ffn(hp, ols, ybp, sgp, woa, wob, wout, *ln2, f2, alpha, seq)
        for g in range(N_GROUPS):
            outs[g].append(_to_window(wins[g]))

        ols_s = []
        for g, (win_cm, ol) in enumerate(side_b + side_a):
            lg = win_cm.shape[2]
            win = jnp.transpose(win_cm.reshape(dec_batch, 2, GROUP_HEADS, HEAD_DIM, lg), (0, 4, 1, 2, 3))
            outs[3 + g].append(win)
            ols_s.append(ol.reshape(1, 1, dec_batch * dec_seq, 2 * GW))
        ys = _mix_ffn(hs, ols_s, ybs, sgs, woa, wob, wout, *ln2, f2, alpha, dec_batch * dec_seq)
        outs[6].append(vn.reshape(dec_batch, dec_seq, SGU_W))

    return (yp.reshape(batch, seq, D_MODEL), ys.reshape(dec_batch, dec_seq, D_MODEL),
            *[jnp.stack(o) for o in outs])
```

```python
import functools
import math
from typing import Callable, NamedTuple

import jax
import jax.numpy as jnp
import numpy as np
from jax import lax
from jax.experimental import pallas as pl
from jax.experimental.pallas import tpu as pltpu

F32 = jnp.float32
BF16 = jnp.bfloat16

D_MODEL = 1024
HEAD_DIM = 64
GROUP_HEADS = 4
ATT_GROUPS = ((128, 1), (512, 4), (2048, 16))
N_GROUPS = len(ATT_GROUPS)
N_ATT_HEADS = N_GROUPS * GROUP_HEADS
ATT_W = N_ATT_HEADS * HEAD_DIM
GW = GROUP_HEADS * HEAD_DIM
BLK = 128
N_BUCKETS = 32
MAX_DISTANCE = 2048
SGU_CHUNK = 128
SGU_GROUPS = 4
SGU_GROUP_W = 128
SGU_W = SGU_GROUPS * SGU_GROUP_W
D_FF = 2816
LN_EPS = 1e-5
NEG = -1e30
SCALE = HEAD_DIM ** -0.5

TM = 512
FC = 256
NC = D_FF // FC
LN_PARTS = 2
ATTN_UNROLL = 16
ATTN_UNROLL_SINGLE = 4
VMEM_LIMIT = 60 * 1024 * 1024


def _t5_bucket(dist):
    dist = np.asarray(dist, np.int64)
    max_exact = N_BUCKETS // 2
    large = max_exact + (np.log(np.maximum(dist, max_exact) / max_exact)
                         / np.log(MAX_DISTANCE / max_exact) * (N_BUCKETS - max_exact)).astype(np.int64)
    large = np.minimum(large, N_BUCKETS - 1)
    return np.where(dist < max_exact, dist, large).astype(np.int32)


def _ln(t, g, b):
    mu = jnp.mean(t, axis=-1, keepdims=True)
    d = t - mu
    var = jnp.mean(d * d, axis=-1, keepdims=True)
    return d * lax.rsqrt(var + LN_EPS) * g + b


def _whole(shape):
    nd = len(shape)
    return pl.BlockSpec(shape, lambda *_: (0,) * nd, pipeline_mode=pl.Buffered(1))


class _Side(NamedTuple):
    name: str
    init: Callable
    step: Callable
    inputs: tuple
    in_specs: tuple
    out_shape: tuple
    out_specs: tuple
    scratch_shapes: tuple


def _take(refs, pos, count):
    return refs[pos:pos + count], pos + count


def _call(name, main, main_init, grid, inputs, in_specs, out_shape, out_specs, scratch, sides=()):
    n_in, n_out, n_scr = len(inputs), len(out_shape), len(scratch)

    def body(*refs):
        refs = list(refs)
        m_in, pos = _take(refs, 0, n_in)
        s_in = []
        for s in sides:
            r, pos = _take(refs, pos, len(s.inputs))
            s_in.append(r)
        m_out, pos = _take(refs, pos, n_out)
        s_out = []
        for s in sides:
            r, pos = _take(refs, pos, len(s.out_shape))
            s_out.append(r)
        m_scr, pos = _take(refs, pos, n_scr)
        s_scr = []
        for s in sides:
            r, pos = _take(refs, pos, len(s.scratch_shapes))
            s_scr.append(r)

        if main_init is not None or sides:
            first = pl.program_id(0) == 0
            for axis in range(1, len(grid)):
                first = jnp.logical_and(first, pl.program_id(axis) == 0)

            @pl.when(first)
            def _():
                if main_init is not None:
                    main_init(m_in, m_out, m_scr)
                for s, i, o, c in zip(sides, s_in, s_out, s_scr):
                    s.init(i, o, c)

        main(m_in, m_out, m_scr)
        for s, i, o, c in zip(sides, s_in, s_out, s_scr):
            s.step(i, o, c)

    all_in = list(inputs) + [a for s in sides for a in s.inputs]
    all_in_specs = list(in_specs) + [a for s in sides for a in s.in_specs]
    all_out_shape = list(out_shape) + [a for s in sides for a in s.out_shape]
    all_out_specs = list(out_specs) + [a for s in sides for a in s.out_specs]
    all_scratch = list(scratch) + [a for s in sides for a in s.scratch_shapes]
    res = pl.pallas_call(
        body,
        grid=grid,
        in_specs=all_in_specs,
        out_specs=all_out_specs,
        out_shape=all_out_shape,
        scratch_shapes=all_scratch,
        compiler_params=pltpu.CompilerParams(dimension_semantics=("arbitrary",) * len(grid),
                                             vmem_limit_bytes=VMEM_LIMIT),
        name=name + "".join("_" + s.name for s in sides),
    )(*all_in)
    res = list(res)
    main_res, pos = _take(res, 0, n_out)
    side_res = []
    for s in sides:
        r, pos = _take(res, pos, len(s.out_shape))
        side_res.append(r)
    return main_res, side_res


def _swiglu_ln(alpha, x_ref, xb_ref, w_refs, hid_ref, o_ref):
    w1_ref, w3_ref, w2_ref, g_ref, b_ref = w_refs
    tm = x_ref.shape[0]
    for c in range(NC):
        cols = slice(c * FC, (c + 1) * FC)
        parts = LN_PARTS if c == 0 else 1
        for rows in [slice(r, r + tm // parts) for r in range(0, tm, tm // parts)]:
            a = jnp.dot(xb_ref[rows, :], w1_ref[:, cols], preferred_element_type=F32)
            b = jnp.dot(xb_ref[rows, :], w3_ref[:, cols], preferred_element_type=F32)
            hid_ref[rows, cols] = ((a * jax.nn.sigmoid(a)) * b).astype(BF16)
    for rows in [slice(r, r + tm // LN_PARTS) for r in range(0, tm, tm // LN_PARTS)]:
        ffn = jnp.dot(hid_ref[rows, :], w2_ref[...], preferred_element_type=F32)
        t = alpha * x_ref[rows, :] + 0.5 * ffn
        o_ref[rows, :] = _ln(t, g_ref[...], b_ref[...])


N_FFN_W = 5


def _ffn_specs():
    return [_whole((D_MODEL, D_FF)), _whole((D_MODEL, D_FF)), _whole((D_FF, D_MODEL)),
            _whole((1, D_MODEL)), _whole((1, D_MODEL))]


def _ffn_ln_main(alpha, in_refs, out_refs, scratch_refs):
    x_ref = in_refs[0]
    xb_ref, hid_ref = scratch_refs
    xb_ref[...] = x_ref[...].astype(BF16)
    _swiglu_ln(alpha, x_ref, xb_ref, in_refs[1:1 + N_FFN_W], hid_ref, out_refs[0])


def _ffn_ln(x2d, ffn_w, alpha, sides=()):
    n = x2d.shape[0]
    assert n % TM == 0 and len(ffn_w) == N_FFN_W
    (out,), side_res = _call(
        "ffn_ln", functools.partial(_ffn_ln_main, alpha), None, (n // TM,),
        [x2d, *ffn_w],
        [pl.BlockSpec((TM, D_MODEL), lambda i: (i, 0))] + _ffn_specs(),
        [jax.ShapeDtypeStruct((n, D_MODEL), F32)],
        [pl.BlockSpec((TM, D_MODEL), lambda i: (i, 0))],
        [pltpu.VMEM((TM, D_MODEL), BF16), pltpu.VMEM((TM, D_FF), BF16)],
        sides)
    return out, side_res


def _gelu_exact(z):
    return 0.5 * z * (1.0 + lax.erf(z * (1.0 / math.sqrt(2.0))))


def _uv_branch(hb, wuv_ref, buv_ref, sg_ref, sb_ref):
    zuv = jnp.dot(hb, wuv_ref[...], preferred_element_type=F32) + buv_ref[...]
    zz = _gelu_exact(zuv)
    u = zz[:, :SGU_W]
    vn = _ln(zz[:, SGU_W:], sg_ref[...], sb_ref[...])
    return u, vn


def _gates(hb, wg_ref, bg_ref, sg_out_ref):
    zg = jnp.dot(hb, wg_ref[...], preferred_element_type=F32) + bg_ref[...]
    sg_out_ref[...] = jax.nn.sigmoid(zg).astype(BF16)


def _proj_prompt_main(in_refs, out_refs, scratch_refs):
    (h_ref, wqkv_ref, bqkv_ref, wuv_ref, buv_ref, wg_ref, bg_ref,
     sln_g_ref, sln_b_ref, ws_ref, sbt_ref) = in_refs
    qkv_refs, win_refs = out_refs[0:3], out_refs[3:6]
    yb_ref, sg_ref = out_refs[6:8]
    zs_ref, hb_ref = scratch_refs
    hb_ref[...] = h_ref[...].astype(BF16)
    u, vn = _uv_branch(hb_ref[...], wuv_ref, buv_ref, sln_g_ref, sln_b_ref)
    _gates(hb_ref[...], wg_ref, bg_ref, sg_ref)

    vnb = vn.astype(BF16)
    row = lax.broadcasted_iota(jnp.int32, (SGU_CHUNK, SGU_CHUNK), 0)
    col = lax.broadcasted_iota(jnp.int32, (SGU_CHUNK, SGU_CHUNK), 1)
    for g in range(SGU_GROUPS):
        wm = jnp.where(row >= col, ws_ref[g], 0.0).astype(BF16)
        bias = jnp.broadcast_to(sbt_ref[:, g:g + 1], (SGU_CHUNK, SGU_GROUP_W))
        cs = slice(g * SGU_GROUP_W, (g + 1) * SGU_GROUP_W)
        for c in range(TM // SGU_CHUNK):
            rs = slice(c * SGU_CHUNK, (c + 1) * SGU_CHUNK)
            mixed = jnp.dot(wm, vnb[rs, cs], preferred_element_type=F32) + bias
            yb_ref[rs, cs] = (u[rs, cs] * mixed).astype(BF16)

    for g, (window, dil) in enumerate(ATT_GROUPS):
        zg = jnp.dot(hb_ref[...], wqkv_ref[g], preferred_element_type=F32) + bqkv_ref[g]
        keep = min(window, TM)
        win_refs[g][0] = zg[TM - keep:, GW:]
        zq = jnp.concatenate([zg[:, :GW] * SCALE, zg[:, GW:]], axis=1)
        if dil == 1:
            qkv_refs[g][0, 0] = zq.astype(BF16)
            continue
        ncol = 3 * GW // BLK
        for j in range(ncol):
            zs_ref[j] = zq[:, j * BLK:(j + 1) * BLK]
        rows = TM // dil
        for r in range(dil):
            sub = jnp.concatenate([zs_ref[j, pl.ds(r, rows, stride=dil), :] for j in range(ncol)], axis=1)
            qkv_refs[g][0, r] = sub.astype(BF16)


def _proj_prompt(h2d, batch, seq, wqkv, bqkv, wuv, buv, wg, bg, sln_g, sln_b, ws, sbt, sides=()):
    nt = seq // TM
    assert seq % TM == 0 and TM == ATT_GROUPS[1][0] and TM >= ATT_GROUPS[0][0]
    ins = [h2d, wqkv, bqkv, wuv, buv, wg, bg, sln_g, sln_b, ws, sbt]
    in_specs = [pl.BlockSpec((TM, D_MODEL), lambda b, t: (b * nt + t, 0))] + [_whole(a.shape) for a in ins[1:]]
    out_shape, out_specs = [], []
    for window, dil in ATT_GROUPS:
        out_shape.append(jax.ShapeDtypeStruct((batch, dil, seq // dil, 3 * GW), BF16))
        out_specs.append(pl.BlockSpec((1, dil, TM // dil, 3 * GW), lambda b, t: (b, 0, t, 0)))
    for window, dil in ATT_GROUPS:
        keep = min(window, seq)
        out_shape.append(jax.ShapeDtypeStruct((batch, keep, 2 * GW), F32))
        if keep > TM:
            assert keep == seq
            out_specs.append(pl.BlockSpec((1, TM, 2 * GW), lambda b, t: (b, t, 0)))
        else:
            out_specs.append(pl.BlockSpec((1, keep, 2 * GW), lambda b, t: (b, 0, 0)))
    out_shape += [jax.ShapeDtypeStruct((batch * seq, SGU_W), BF16),
                  jax.ShapeDtypeStruct((batch * seq, 2 * D_MODEL), BF16)]
    out_specs += [pl.BlockSpec((TM, SGU_W), lambda b, t: (b * nt + t, 0)),
                  pl.BlockSpec((TM, 2 * D_MODEL), lambda b, t: (b * nt + t, 0))]
    return _call("proj_prompt", _proj_prompt_main, None, (batch, nt), ins, in_specs, out_shape, out_specs,
                 [pltpu.VMEM((3 * GW // BLK, TM, BLK), F32), pltpu.VMEM((TM, D_MODEL), BF16)], sides)


def _proj_sample_kernel(t_len, ws4_ref, sb4_ref, h_ref, wqkv_ref, bqkv_ref, wuv_ref, buv_ref,
                        wg_ref, bg_ref, sln_g_ref, sln_b_ref,
                        q_ref, kv_ref, yb_ref, sg_ref, vn_ref):
    n = h_ref.shape[0]
    hb = h_ref[...].astype(BF16)
    for g in range(N_GROUPS):
        zg = jnp.dot(hb, wqkv_ref[g], preferred_element_type=F32) + bqkv_ref[g]
        q_ref[:, g * GW:(g + 1) * GW] = zg[:, :GW] * SCALE
        kv_ref[:, g * 2 * GW:(g + 1) * 2 * GW] = zg[:, GW:]

    u, vn = _uv_branch(hb, wuv_ref, buv_ref, sln_g_ref, sln_b_ref)
    vn_ref[...] = vn
    p = lax.broadcasted_iota(jnp.int32, (n, SGU_GROUP_W), 0) & (t_len - 1)
    for g in range(SGU_GROUPS):
        cs = slice(g * SGU_GROUP_W, (g + 1) * SGU_GROUP_W)
        vg = vn[:, cs]
        acc = jnp.zeros((n, SGU_GROUP_W), F32)
        for pp in range(t_len):
            acc = jnp.where(p == pp, sb4_ref[g, pp], acc)
        for d in range(t_len):
            coef = jnp.zeros((n, SGU_GROUP_W), F32)
            for pp in range(d, t_len):
                coef = jnp.where(p == pp, ws4_ref[g, pp * t_len + (pp - d)], coef)
            shifted = vg if d == 0 else pltpu.roll(vg, d, axis=0)
            acc = acc + coef * shifted
        yb_ref[:, cs] = (u[:, cs] * acc).astype(BF16)

    _gates(hb, wg_ref, bg_ref, sg_ref)


def _proj_sample(h2d, t_len, ws4, sb4, wqkv, bqkv, wuv, buv, wg, bg, sln_g, sln_b):
    n = h2d.shape[0]
    assert t_len & (t_len - 1) == 0
    smem = pl.BlockSpec(memory_space=pltpu.SMEM)
    ins = (h2d, wqkv, bqkv, wuv, buv, wg, bg, sln_g, sln_b)
    out_shape = [jax.ShapeDtypeStruct((n, N_GROUPS * GW), F32),
                 jax.ShapeDtypeStruct((n, N_GROUPS * 2 * GW), F32),
                 jax.ShapeDtypeStruct((n, SGU_W), BF16),
                 jax.ShapeDtypeStruct((n, 2 * D_MODEL), BF16),
                 jax.ShapeDtypeStruct((n, SGU_W), F32)]
    return pl.pallas_call(
        functools.partial(_proj_sample_kernel, t_len),
        grid=(1,),
        in_specs=[smem, smem] + [_whole(a.shape) for a in ins],
        out_specs=[_whole(s.shape) for s in out_shape],
        out_shape=out_shape,
        compiler_params=pltpu.CompilerParams(dimension_semantics=("arbitrary",), vmem_limit_bytes=VMEM_LIMIT),
        name="proj_sample",
    )(ws4, sb4, *ins)


def _bias_mask(bmap, buckets, value_of_bucket):
    acc = jnp.full(bmap.shape, NEG, F32)
    for bkt in buckets:
        acc = jnp.where(bmap == bkt, value_of_bucket(bkt), acc)
    return acc


def _prompt_bucket_map(window, dil):
    band = window // dil
    rel = np.arange(BLK)[:, None] - np.arange(2 * BLK)[None, :] + BLK
    valid = (rel >= 0) & (rel <= band)
    return np.where(valid, _t5_bucket(np.clip(rel, 0, band) * dil), -1).astype(np.int32)


def _combine_groups(outs, lses):
    mx = jnp.maximum(jnp.maximum(lses[0], lses[1]), lses[2])
    es = [jnp.exp(ls - mx) for ls in lses]
    den = es[0] + es[1] + es[2]
    num = es[0] * outs[0] + es[1] * outs[1] + es[2] * outs[2]
    return num * (1.0 / den)


def _attn_prompt_kernel(buckets, tab_ref, bmap_ref, qkv0_ref, qkv1_ref, qkv2_ref,
                        ol0_ref, ol1_ref, ol2_ref, bm_ref):
    @pl.when(pl.program_id(0) == 0)
    def _():
        for g in range(N_GROUPS):
            bmap = bmap_ref[g]
            for h in range(GROUP_HEADS):
                rs = slice(h * BLK, (h + 1) * BLK)
                bm = _bias_mask(bmap, buckets[g], lambda bkt: tab_ref[bkt, g * GROUP_HEADS + h])
                bm_ref[2 * g, rs, :] = bm
                bm_ref[2 * g + 1, rs, :BLK] = bm[:, BLK:]
                bm_ref[2 * g + 1, rs, BLK:] = jnp.full((BLK, BLK), NEG, F32)

    lane_head = lax.broadcasted_iota(jnp.int32, (BLK, GW), 1) >> 6
    lane_lo = lax.broadcasted_iota(jnp.int32, (BLK, BLK), 1) < HEAD_DIM

    def unit(g, qkv_ref, ol_ref, r, n):
        single = qkv_ref.shape[2] == BLK
        static_n = isinstance(n, int)
        q0 = n * BLK if static_n else pl.multiple_of(n * BLK, BLK)
        q = qkv_ref[0, r, pl.ds(q0, BLK), 0:GW]
        qs = jnp.concatenate([jnp.where(lane_head == h, q, jnp.zeros_like(q))
                              for h in range(GROUP_HEADS)], axis=0)
        if single:
            k = qkv_ref[0, r, :, GW:2 * GW]
            v = qkv_ref[0, r, :, 2 * GW:3 * GW]
            bm = bm_ref[2 * g, :, BLK:]
        else:
            if static_n:
                k0 = max(q0 - BLK, 0)
                bm = bm_ref[2 * g + int(n == 0)]
            else:
                k0 = pl.multiple_of(jnp.maximum(q0 - BLK, 0), BLK)
                bm = bm_ref[2 * g + (n == 0).astype(jnp.int32)]
            k = qkv_ref[0, r, pl.ds(k0, 2 * BLK), GW:2 * GW]
            v = qkv_ref[0, r, pl.ds(k0, 2 * BLK), 2 * GW:3 * GW]
        s = lax.dot_general(qs, k, (((1,), (1,)), ((), ())), preferred_element_type=F32) + bm
        m = jnp.max(s, axis=-1, keepdims=True)
        p = jnp.exp(s - m)
        l = jnp.sum(p, axis=-1, keepdims=True)
        pb = p.astype(BF16)
        linv = 1.0 / l
        lse = m + jnp.log(l)
        rows = pl.ds(q0, BLK)
        for half in range(2):
            pv = jnp.dot(pb[2 * half * BLK:(2 * half + 2) * BLK], v[:, half * BLK:(half + 1) * BLK],
                         preferred_element_type=F32)
            ra = slice(2 * half * BLK, (2 * half + 1) * BLK)
            rb = slice((2 * half + 1) * BLK, (2 * half + 2) * BLK)
            ol_ref[0, r, rows, half * BLK:(half + 1) * BLK] = jnp.where(
                lane_lo, pv[:BLK] * linv[ra], pv[BLK:] * linv[rb])
            ol_ref[0, r, rows, GW + half * BLK:GW + (half + 1) * BLK] = jnp.where(lane_lo, lse[ra], lse[rb])

    for g, (qkv_ref, ol_ref) in enumerate(((qkv0_ref, ol0_ref), (qkv1_ref, ol1_ref), (qkv2_ref, ol2_ref))):
        dil, sub_len = qkv_ref.shape[1], qkv_ref.shape[2]
        nb = sub_len // BLK
        n_units = dil * nb
        unroll = ATTN_UNROLL_SINGLE if nb == 1 else ATTN_UNROLL
        assert n_units % unroll == 0 and (nb == 1 or unroll % nb == 0 or dil == 1)

        def trip(t, carry, g=g, qkv_ref=qkv_ref, ol_ref=ol_ref, nb=nb, dil=dil, unroll=unroll):
            for j in range(unroll):
                if nb == 1:
                    r, n = t * unroll + j, 0
                elif dil == 1:
                    r, n = 0, t * unroll + j
                else:
                    r, n = t * (unroll // nb) + j // nb, j % nb
                unit(g, qkv_ref, ol_ref, r, n)
            return carry

        lax.fori_loop(0, n_units // unroll, trip, 0)


def _attn_prompt(tab, qkvs, batch):
    bmaps = [_prompt_bucket_map(w, d) for w, d in ATT_GROUPS]
    buckets = tuple(tuple(int(b) for b in np.unique(m) if b >= 0) for m in bmaps)
    bmap = jnp.asarray(np.stack(bmaps))
    in_specs = [pl.BlockSpec(memory_space=pltpu.SMEM), _whole(bmap.shape)]
    out_shape, out_specs = [], []
    for a in qkvs:
        _, dil, sub_len, _ = a.shape
        in_specs.append(pl.BlockSpec((1, dil, sub_len, 3 * GW), lambda b: (b, 0, 0, 0)))
        out_shape.append(jax.ShapeDtypeStruct((batch, dil, sub_len, 2 * GW), F32))
        out_specs.append(pl.BlockSpec((1, dil, sub_len, 2 * GW), lambda b: (b, 0, 0, 0)))
    return pl.pallas_call(
        functools.partial(_attn_prompt_kernel, buckets),
        grid=(batch,),
        in_specs=in_specs,
        out_specs=out_specs,
        out_shape=out_shape,
        scratch_shapes=[pltpu.VMEM((2 * N_GROUPS, GROUP_HEADS * BLK, 2 * BLK), F32)],
        compiler_params=pltpu.CompilerParams(dimension_semantics=("arbitrary",), vmem_limit_bytes=VMEM_LIMIT),
        name="attn_prompt",
    )(tab, bmap, *qkvs)


def _attn_sample_init(g, t_len, buckets, in_refs, out_refs, scratch_refs):
    tab_ref, bmap_ref = in_refs[:2]
    bm_ref, tr_ref, _ = scratch_refs
    rh =lax.broadcasted_iota(jnp.int32, bm_ref.shape, 0) // t_len

    def value(bkt):
        v = jnp.full(bm_ref.shape, tab_ref[bkt, g * GROUP_HEADS], F32)
        for h in range(1, GROUP_HEADS):
            v = jnp.where(rh == h, tab_ref[bkt, g * GROUP_HEADS + h], v)
        return v

    bm_ref[...] = _bias_mask(bmap_ref[...], buckets, value)
    tr_ref[...] = jnp.zeros_like(tr_ref)


def _attn_sample_step(t_len, in_refs, out_refs, scratch_refs):
    _, _, q_ref, kvnew_ref, st_ref = in_refs
    win_ref, ol_ref = out_refs
    bm_ref, tr_ref, fb_ref = scratch_refs
    lg = st_ref.shape[2]
    nrow = GROUP_HEADS * t_len
    row_head = lax.broadcasted_iota(jnp.int32, (nrow, GW), 0) // t_len
    lane_head = lax.broadcasted_iota(jnp.int32, (nrow, GW), 1) >> 6
    ext = lg + BLK
    sel = row_head == lane_head
    for i in range(st_ref.shape[0]):
        tr_ref[i, 0:t_len, :] = kvnew_ref[i]
        for c in range(2 * GW // BLK):
            rows = slice(c * BLK, (c + 1) * BLK)
            full = jnp.concatenate([st_ref[i, rows, :], tr_ref[i, :, rows].T], axis=1)
            win_ref[i, rows, :] = pltpu.roll(full, ext - t_len, axis=1)[:, :lg]
            fb_ref[i, rows, :] = full.astype(BF16)

        qs = jnp.where(sel, q_ref[i], 0.0).astype(BF16)
        s = jnp.dot(qs, fb_ref[i, 0:GW, :], preferred_element_type=F32) + bm_ref[...]
        m = jnp.max(s, axis=-1, keepdims=True)
        p = jnp.exp(s - m)
        l = jnp.sum(p, axis=-1, keepdims=True)
        pv = lax.dot_general(p.astype(BF16), fb_ref[i, GW:2 * GW, :], (((1,), (1,)), ((), ())),
                             preferred_element_type=F32) * (1.0 / l)
        lse = jnp.broadcast_to(m + jnp.log(l), (nrow, GW))
        o = jnp.where(sel, pv, 0.0)
        ls = jnp.where(sel, lse, 0.0)
        osum, lsum = o, ls
        for h in range(1, GROUP_HEADS):
            osum = osum + pltpu.roll(o, h * t_len, axis=0)
            lsum = lsum + pltpu.roll(ls, h * t_len, axis=0)
        ol_ref[i, :, 0:GW] = osum[0:t_len]
        ol_ref[i, :, GW:2 * GW] = lsum[0:t_len]


def _sample_bucket_map(window, dil, lg, t_len):
    band = window // dil
    ext = lg + BLK
    bmap = np.full((t_len, ext), -1, np.int64)
    t = np.arange(t_len)[:, None]
    j = np.arange(lg + t_len)[None, :]
    dist = lg + t - j
    valid = (dist >= 0) & (dist % dil == 0) & (dist <= band * dil)
    bmap[:, :lg + t_len] = np.where(valid, _t5_bucket(np.maximum(dist, 0)), -1)
    return np.tile(bmap, (GROUP_HEADS, 1)).astype(np.int32)


def _attn_sample_side(g, tab, q16, kvnew, st, n_steps, step_of):
    window, dil = ATT_GROUPS[g]
    nb, _, lg = st.shape
    assert lg == window and nb % n_steps == 0
    per = nb // n_steps
    t_len = kvnew.shape[1]
    nrow = GROUP_HEADS * t_len
    bmap_np = _sample_bucket_map(window, dil, lg, t_len)
    buckets = tuple(int(b) for b in np.unique(bmap_np) if b >= 0)
    bmap = jnp.asarray(bmap_np)

    def blk(*idx):
        return (step_of(*idx), 0, 0)

    return _Side(
        name="attn_sample_g%d" % g,
        init=functools.partial(_attn_sample_init, g, t_len, buckets),
        step=functools.partial(_attn_sample_step, t_len),
        inputs=(tab, bmap, q16, kvnew, st),
        in_specs=(pl.BlockSpec(memory_space=pltpu.SMEM), _whole(bmap.shape),
                  pl.BlockSpec((per, nrow, GW), blk),
                  pl.BlockSpec((per, t_len, 2 * GW), blk),
                  pl.BlockSpec((per, 2 * GW, lg), blk)),
        out_shape=(jax.ShapeDtypeStruct((nb, 2 * GW, lg), F32),
                   jax.ShapeDtypeStruct((nb, t_len, 2 * GW), F32)),
        out_specs=(pl.BlockSpec((per, 2 * GW, lg), blk),
                   pl.BlockSpec((per, t_len, 2 * GW), blk)),
        scratch_shapes=(pltpu.VMEM((nrow, lg + BLK), F32), pltpu.VMEM((per, BLK, 2 * GW), F32),
                        pltpu.VMEM((per, 2 * GW, lg + BLK), BF16)))


def _mix_ffn_main(alpha, in_refs, out_refs, scratch_refs):
    h_ref = in_refs[0]
    ol_refs = in_refs[1:1 + N_GROUPS]
    yb_ref, sg_ref, woa_ref, wob_ref, wout_ref, g2_ref, b2_ref = in_refs[1 + N_GROUPS:8 + N_GROUPS]
    ffn_refs = in_refs[8 + N_GROUPS:]
    (o_ref,) = out_refs
    u_ref, h2_ref, h2b_ref, hid_ref = scratch_refs
    tm = h_ref.shape[0]
    ncol = 2 * GW // BLK
    ols = []
    for gi, ol_ref in enumerate(ol_refs):
        dil = ol_ref.shape[1]
        if dil == 1:
            ols.append(ol_ref[0, 0])
            continue
        for r in range(dil):
            for j in range(ncol):
                u_ref[gi, j, pl.ds(r, tm // dil, stride=dil), :] = ol_ref[0, r, :, j * BLK:(j + 1) * BLK]
        ols.append(jnp.concatenate([u_ref[gi, j] for j in range(ncol)], axis=1))
    att = _combine_groups([ol[:, :GW] for ol in ols], [ol[:, GW:] for ol in ols]).astype(BF16)
    a = jnp.dot(att, woa_ref[...], preferred_element_type=F32)
    bb = jnp.dot(yb_ref[...], wob_ref[...], preferred_element_type=F32)
    gated = sg_ref[:, :D_MODEL].astype(F32) * a + sg_ref[:, D_MODEL:].astype(F32) * bb
    gb = gated.astype(BF16)
    for rows in [slice(r, r + tm // LN_PARTS) for r in range(0, tm, tm // LN_PARTS)]:
        mix = jnp.dot(gb[rows], wout_ref[...], preferred_element_type=F32)
        h2 = _ln(alpha * h_ref[rows, :] + mix, g2_ref[...], b2_ref[...])
        h2_ref[rows, :] = h2
        h2b_ref[rows, :] = h2.astype(BF16)
    _swiglu_ln(alpha, h2_ref, h2b_ref, ffn_refs, hid_ref, o_ref)


def _mix_ffn(h2d, ols, yb, sg, woa, wob, wout, g2, b2, ffn_w, alpha, seq):
    n = h2d.shape[0]
    assert n % TM == 0 and seq % TM == 0 and len(ffn_w) == N_FFN_W
    nt = seq // TM

    def rows(a):
        return pl.BlockSpec((TM, a.shape[1]), lambda i: (i, 0))

    ins = [h2d, *ols, yb, sg, woa, wob, wout, g2, b2, *ffn_w]
    in_specs = [rows(h2d)]
    for a in ols:
        dil = a.shape[1]
        in_specs.append(pl.BlockSpec((1, dil, TM // dil, 2 * GW), lambda i: (i // nt, 0, i % nt, 0)))
    in_specs += ([rows(yb), rows(sg)] + [_whole(a.shape) for a in (woa, wob, wout, g2, b2)] + _ffn_specs())
    (out,), _ = _call(
        "mix_ffn", functools.partial(_mix_ffn_main, alpha), None, (n // TM,), ins, in_specs,
        [jax.ShapeDtypeStruct((n, D_MODEL), F32)], [rows(h2d)],
        [pltpu.VMEM((N_GROUPS, 2 * GW // BLK, TM, BLK), F32), pltpu.VMEM((TM, D_MODEL), F32),
         pltpu.VMEM((TM, D_MODEL), BF16), pltpu.VMEM((TM, D_FF), BF16)])
    return out


def _prep_ffn(w1, w3, w2, ln_g, ln_b):
    return (w1.astype(BF16), w3.astype(BF16), w2.astype(BF16), ln_g[None, :], ln_b[None, :])


def _prep_w_in(w_in, b_in):
    def cols(a, lo, n):
        return a[..., lo:lo + n]
    wq, bq = [], []
    for g in range(N_GROUPS):
        parts = [(i * ATT_W + g * GW, GW) for i in range(3)]
        wq.append(jnp.concatenate([cols(w_in, lo, n) for lo, n in parts], axis=-1))
        bq.append(jnp.concatenate([cols(b_in, lo, n) for lo, n in parts], axis=-1))
    o_uv = 3 * ATT_W
    return (jnp.stack(wq).astype(BF16), jnp.stack(bq)[:, None, :],
            cols(w_in, o_uv, 2 * SGU_W).astype(BF16), cols(b_in, o_uv, 2 * SGU_W)[None, :],
            cols(w_in, o_uv + 2 * SGU_W, 2 * D_MODEL).astype(BF16),
            cols(b_in, o_uv + 2 * SGU_W, 2 * D_MODEL)[None, :])


def _to_window(a):
    return a.reshape(a.shape[0], a.shape[1], 2, GROUP_HEADS, HEAD_DIM)


def kernel(x_prompt, x_sample, state_win0, state_win1, state_win2, rel_bias, ln1_g, ln1_b, f1_w1, f1_w3, f1_w2, w_in, b_in, sgu_ln_g, sgu_ln_b, sgu_ws, sgu_b, w_oa, w_ob, w_out, ln2_g, ln2_b, f2_w1, f2_w3, f2_w2, ln3_g, ln3_b):
    depth = ln1_g.shape[0]
    alpha = (2 * depth) ** 0.25
    batch, seq, _ = x_prompt.shape
    dec_batch, dec_seq, _ = x_sample.shape
    states = (state_win0, state_win1, state_win2)

    yp = x_prompt.reshape(batch * seq, D_MODEL)
    ys = x_sample.reshape(dec_batch * dec_seq, D_MODEL)
    outs = [[] for _ in range(7)]
    for l in range(depth):
        f1 = _prep_ffn(f1_w1[l], f1_w3[l], f1_w2[l], ln1_g[l], ln1_b[l])
        f2 = _prep_ffn(f2_w1[l], f2_w3[l], f2_w2[l], ln3_g[l], ln3_b[l])
        wqkv, bqkv, wuv, buv, wg, bg = _prep_w_in(w_in[l], b_in[l])
        sln = (sgu_ln_g[l][None, :], sgu_ln_b[l][None, :])
        woa, wob, wout = w_oa[l].astype(BF16), w_ob[l].astype(BF16), w_out[l].astype(BF16)
        ln2 = (ln2_g[l][None, :], ln2_b[l][None, :])

        hs, _ = _ffn_ln(ys, f1, alpha)
        ws4 = sgu_ws[l][:, :dec_seq, :dec_seq].reshape(SGU_GROUPS, dec_seq * dec_seq)
        sb4 = sgu_b[l][:, :dec_seq]
        qs, kvs, ybs, sgs, vn = _proj_sample(hs, dec_seq, ws4, sb4, wqkv, bqkv, wuv, buv, wg, bg, *sln)
        nt = seq // TM
        n_steps = batch * nt
        sides = []
        for g in range(N_GROUPS):
            st = states[g][l]
            lg = st.shape[1]
            st_cm = jnp.transpose(st, (0, 2, 3, 4, 1)).reshape(dec_batch, 2 * GW, lg)
            q16 = jnp.tile(qs[:, g * GW:(g + 1) * GW].reshape(dec_batch, 1, dec_seq, GW),
                           (1, GROUP_HEADS, 1, 1)).reshape(dec_batch, GROUP_HEADS * dec_seq, GW)
            kvnew = kvs[:, g * 2 * GW:(g + 1) * 2 * GW].reshape(dec_batch, dec_seq, 2 * GW)
            step_of = (lambda i: i) if g == N_GROUPS - 1 else (lambda b, t: b * nt + t)
            sides.append(_attn_sample_side(g, rel_bias, q16, kvnew, st_cm, n_steps, step_of))

        hp, side_a = _ffn_ln(yp, f1, alpha, sides=sides[2:])
        res, side_b = _proj_prompt(hp, batch, seq, wqkv, bqkv, wuv, buv, wg, bg, *sln,
                                   sgu_ws[l], jnp.transpose(sgu_b[l]), sides=sides[:2])
        qkvs, wins, ybp, sgp = res[0:3], res[3:6], res[6], res[7]
        ols = _attn_prompt(rel_bias, qkvs, batch)
        yp = _mix_ffn(hp, ols, ybp, sgp, woa, wob, wout, *ln2, f2, alpha, seq)
        for g in range(N_GROUPS):
            outs[g].append(_to_window(wins[g]))

        ols_s = []
        for g, (win_cm, ol) in enumerate(side_b + side_a):
            lg = win_cm.shape[2]
            win = jnp.transpose(win_cm.reshape(dec_batch, 2, GROUP_HEADS, HEAD_DIM, lg), (0, 4, 1, 2, 3))
            outs[3 + g].append(win)
            ols_s.append(ol.reshape(1, 1, dec_batch * dec_seq, 2 * GW))
        ys = _mix_ffn(hs, ols_s, ybs, sgs, woa, wob, wout, *ln2, f2, alpha, dec_batch * dec_seq)
        outs[6].append(vn.reshape(dec_batch, dec_seq, SGU_W))

    return (yp.reshape(batch, seq, D_MODEL), ys.reshape(dec_batch, dec_seq, D_MODEL),
            *[jnp.stack(o) for o in outs])
```
